```python
import math
import jax, jax.numpy as jnp
from jax import lax
import numpy as np

D_MODEL = 2048
BATCH = 2
SEQ = 4096
DEPTH = 1

N_META = 16
RWKV_HEAD = 64
RWKV_HEADS = D_MODEL // RWKV_HEAD
RWKV_DIM = RWKV_HEADS * RWKV_HEAD
DECAY_LORA = 96
ICL_LORA = 96
GATE_LORA = 256
DECAY_SCALE = math.exp(-0.5)
GN_EPS = 64e-5
POOL_WINDOWS = (2, 4, 8, 16)
POOL_GROUPS = len(POOL_WINDOWS)
POOL_DIM = D_MODEL // 2
POOL_GROUP_DIM = POOL_DIM // POOL_GROUPS
N_BRANCHES = 2
RWKV_SPLITS = (RWKV_DIM, RWKV_DIM, RWKV_DIM, DECAY_LORA, ICL_LORA, GATE_LORA)
RWKV_COLS = sum(RWKV_SPLITS)
IN_COLS = RWKV_COLS + POOL_DIM + N_BRANCHES * D_MODEL
PEER_HEADS = 8
PEER_TOPK = 16
N_KEYS = 128
N_EXPERTS = N_KEYS * N_KEYS
PEER_KEY_DIM = 256
PEER_HALF = PEER_KEY_DIM // 2
PEER_BLOCK = 128
DN_ALPHA = (2.0 * DEPTH) ** 0.25
DN_BETA = (8.0 * DEPTH) ** -0.25
LN_EPS = 1e-5

kernel_name = 'hybrid_rwkv7_pool_peer_deepnorm'


def layer_norm(x, g, b):
    xf = x.astype(jnp.float32)
    mu = jnp.mean(xf, axis=-1, keepdims=True)
    var = jnp.mean(jnp.square(xf - mu), axis=-1, keepdims=True)
    return ((xf - mu) * lax.rsqrt(var + LN_EPS) * g + b).astype(x.dtype)


def token_shift(u):
    return jnp.pad(u[:, :-1], ((0, 0), (1, 0), (0, 0)))


def rwkv7_scan(r, w, k, v, kk, a):
    def step(S, inp):
        r_t, w_t, k_t, v_t, kk_t, a_t = inp
        sa = jnp.einsum('bhij,bhj->bhi', S, -kk_t)
        S = (S * w_t[:, :, None, :]
             + sa[..., None] * (kk_t * a_t)[:, :, None, :]
             + v_t[..., None] * k_t[:, :, None, :])
        y_t = jnp.einsum('bhij,bhj->bhi', S, r_t)
        return S, y_t
    B, T, H, N = r.shape
    xs = tuple(jnp.moveaxis(z, 1, 0) for z in (r, w, k, v, kk, a))
    S0 = jnp.zeros((B, H, N, N), jnp.float32)
    _, y = lax.scan(step, S0, xs)
    return jnp.moveaxis(y, 0, 1)


def rwkv7_branch(p, mu_shift, w0, w_w2, a0, w_a2, w_g2, k_k, k_a, r_k, gn_g, gn_b):
    B, T, _ = p.shape
    pf = p.astype(jnp.float32)
    pf = pf + (token_shift(pf) - pf) * mu_shift
    cuts = np.cumsum(RWKV_SPLITS)[:-1].tolist()
    r, k, v, wl, al, gl = jnp.split(pf, cuts, axis=-1)
    decay = jnp.exp(-DECAY_SCALE * jax.nn.sigmoid(w0 + jnp.tanh(wl) @ w_w2))
    a = jax.nn.sigmoid(a0 + al @ w_a2)
    g = jax.nn.sigmoid(gl) @ w_g2
    heads = lambda z: z.reshape(B, T, RWKV_HEADS, RWKV_HEAD)
    kk = heads(k * k_k)
    kk = kk / jnp.maximum(jnp.linalg.norm(kk, axis=-1, keepdims=True), 1e-12)
    k = k * (1.0 + (a - 1.0) * k_a)
    r_h, k_h, v_h, a_h, w_h = heads(r), heads(k), heads(v), heads(a), heads(decay)
    y = rwkv7_scan(r_h, w_h, k_h, v_h, kk, a_h)
    ym = jnp.mean(y, axis=-1, keepdims=True)
    yv = jnp.mean(jnp.square(y - ym), axis=-1, keepdims=True)
    y = ((y - ym) * lax.rsqrt(yv + GN_EPS)).reshape(B, T, RWKV_DIM) * gn_g + gn_b
    bonus = jnp.sum(r_h * k_h * r_k, axis=-1, keepdims=True) * v_h
    y = (y + bonus.reshape(B, T, RWKV_DIM)) * g
    return y.astype(p.dtype)


def causal_pool_minus_identity(u, window):
    T = u.shape[1]
    c = jnp.pad(jnp.cumsum(u, axis=1), ((0, 0), (1, 0), (0, 0)))
    t = jnp.arange(T)
    lo = jnp.maximum(t + 1 - window, 0)
    cnt = jnp.minimum(t + 1, window).astype(jnp.float32)
    return (c[:, 1:] - c[:, lo]) / cnt[None, :, None] - u


def pool_branch(u, w_pool, pool_scale):
    B, T, _ = u.shape
    ug = u.astype(jnp.float32).reshape(B, T, POOL_GROUPS, POOL_GROUP_DIM)
    pooled = jnp.stack([causal_pool_minus_identity(ug[:, :, gi], win)
                        for gi, win in enumerate(POOL_WINDOWS)], axis=2)
    mixed = jnp.einsum('btgc,gcd->btgd', pooled, w_pool).reshape(B, T, POOL_DIM)
    return (mixed * pool_scale).astype(u.dtype)


def peer_ffn(h, w_q, sub_keys, u_table, v_table):
    B, T, D = h.shape
    xt = h.reshape(B * T, D)
    n = xt.shape[0]
    q = (xt @ w_q).reshape(n, PEER_HEADS, 2, PEER_HALF)
    s = jnp.einsum('nhpc,pkc->nhpk', q, sub_keys).astype(jnp.float32)
    sv, si = lax.top_k(s, PEER_TOPK)
    cand = (sv[:, :, 0, :, None] + sv[:, :, 1, None, :]).reshape(n, PEER_HEADS, PEER_TOPK * PEER_TOPK)
    cand_idx = (si[:, :, 0, :, None] * N_KEYS + si[:, :, 1, None, :]).reshape(n, PEER_HEADS, PEER_TOPK * PEER_TOPK)
    top_s, pos = lax.top_k(cand, PEER_TOPK)
    experts = jnp.take_along_axis(cand_idx, pos, axis=-1)
    gates = jax.nn.softmax(top_s, axis=-1)
    n_pad = (-n) % PEER_BLOCK
    xb = jnp.pad(xt, ((0, n_pad), (0, 0))).reshape(-1, PEER_BLOCK, D)
    eb = jnp.pad(experts, ((0, n_pad), (0, 0), (0, 0))).reshape(-1, PEER_BLOCK, PEER_HEADS, PEER_TOPK)
    gb = jnp.pad(gates, ((0, n_pad), (0, 0), (0, 0))).reshape(-1, PEER_BLOCK, PEER_HEADS, PEER_TOPK)

    def block(args):
        xblk, eblk, gblk = args
        u = u_table[eblk]
        act = jax.nn.gelu(jnp.einsum('nd,nhkd->nhk', xblk, u).astype(jnp.float32), approximate=False) * gblk
        v = v_table[eblk]
        return jnp.einsum('nhk,nhkd->nd', act.astype(v.dtype), v)

    out = lax.map(block, (xb, eb, gb)).reshape(-1, D)[:n]
    return out.reshape(B, T, D).astype(h.dtype)


def setup_inputs(seed: int = 0) -> dict:
    key = jax.random.key(seed)
    ks = iter(jax.random.split(key, 40))
    nrm = lambda shape, scale: jax.random.normal(next(ks), shape, jnp.float32) * scale
    uni = lambda shape, lo, hi: jax.random.uniform(next(ks), shape, jnp.float32, lo, hi)
    L, D = DEPTH, D_MODEL
    return {
        'x': nrm((BATCH, SEQ, D), 1.0),
        'meta_tokens': nrm((N_META, D), 1.0),
        'ln_in_g': 1.0 + nrm((D,), 0.05),
        'ln_in_b': nrm((D,), 0.05),
        'w_in': nrm((L, D, IN_COLS), D ** -0.5),
        'mu_shift': uni((L, RWKV_COLS), 0.0, 1.0),
        'w0': uni((L, RWKV_DIM), -4.0, 2.0),
        'w_w2': nrm((L, DECAY_LORA, RWKV_DIM), DECAY_LORA ** -0.5),
        'a0': nrm((L, RWKV_DIM), 0.5),
        'w_a2': nrm((L, ICL_LORA, RWKV_DIM), ICL_LORA ** -0.5),
        'w_g2': nrm((L, GATE_LORA, RWKV_DIM), GATE_LORA ** -0.5),
        'k_k': 0.85 + nrm((L, RWKV_DIM), 0.05),
        'k_a': 1.0 + nrm((L, RWKV_DIM), 0.05),
        'r_k': nrm((L, RWKV_HEADS, RWKV_HEAD), 0.1),
        'gn_g': 1.0 + nrm((L, RWKV_DIM), 0.05),
        'gn_b': nrm((L, RWKV_DIM), 0.05),
        'w_pool': nrm((L, POOL_GROUPS, POOL_GROUP_DIM, POOL_GROUP_DIM), POOL_GROUP_DIM ** -0.5),
        'pool_scale': 1.0 + nrm((L, POOL_DIM), 0.1),
        'b_gate': nrm((L, N_BRANCHES, D), 0.1),
        'w_branch_a': nrm((L, RWKV_DIM, D), RWKV_DIM ** -0.5),
        'w_branch_b': nrm((L, POOL_DIM, D), POOL_DIM ** -0.5),
        'w_out': nrm((L, D, D), DN_BETA * D ** -0.5),
        'ln1_g': 1.0 + nrm((L, D), 0.05),
        'ln1_b': nrm((L, D), 0.05),
        'w_q_peer': nrm((L, D, PEER_HEADS * PEER_KEY_DIM), D ** -0.5),
        'sub_keys': nrm((L, 2, N_KEYS, PEER_HALF), PEER_HALF ** -0.5),
        'u_table': nrm((L, N_EXPERTS, D), D ** -0.5),
        'v_table': nrm((L, N_EXPERTS, D), DN_BETA),
        'ln2_g': 1.0 + nrm((L, D), 0.05),
        'ln2_b': nrm((L, D), 0.05),
    }


def reference(x, meta_tokens, ln_in_g, ln_in_b, w_in, mu_shift, w0, w_w2, a0, w_a2, w_g2,
              k_k, k_a, r_k, gn_g, gn_b, w_pool, pool_scale, b_gate, w_branch_a, w_branch_b,
              w_out, ln1_g, ln1_b, w_q_peer, sub_keys, u_table, v_table, ln2_g, ln2_b):
    B = x.shape[0]
    meta = jnp.broadcast_to(meta_tokens[None].astype(x.dtype), (B, N_META, D_MODEL))
    h = layer_norm(jnp.concatenate([meta, x], axis=1), ln_in_g, ln_in_b)
    T = h.shape[1]
    for l in range(DEPTH):
        p = h @ w_in[l]
        p_rwkv = p[..., :RWKV_COLS]
        p_pool = p[..., RWKV_COLS:RWKV_COLS + POOL_DIM]
        p_gate = p[..., RWKV_COLS + POOL_DIM:].reshape(B, T, N_BRANCHES, D_MODEL)
        ya = rwkv7_branch(p_rwkv, mu_shift[l], w0[l], w_w2[l], a0[l], w_a2[l], w_g2[l],
                          k_k[l], k_a[l], r_k[l], gn_g[l], gn_b[l])
        yb = pool_branch(p_pool, w_pool[l], pool_scale[l])
        gate = jax.nn.sigmoid(p_gate + b_gate[l])
        merged = gate[:, :, 0] * (ya @ w_branch_a[l]) + gate[:, :, 1] * (yb @ w_branch_b[l])
        h = layer_norm(DN_ALPHA * h + merged @ w_out[l], ln1_g[l], ln1_b[l])
        h = layer_norm(DN_ALPHA * h + peer_ffn(h, w_q_peer[l], sub_keys[l], u_table[l], v_table[l]),
                       ln2_g[l], ln2_b[l])
    return h[:, N_META:]
```

```python
import functools
import math

import jax
import jax.numpy as jnp
from jax import lax
from jax.experimental import pallas as pl
from jax.experimental.pallas import tpu as pltpu

F32 = jnp.float32
BF16 = jnp.bfloat16

LANE = 128
N_META = 16
RWKV_HEAD = 64
CHUNK = 64
DECAY_LORA = 96
ICL_LORA = 96
GATE_LORA = 256
LORA_PAD = 128
LORA_W = 2 * LORA_PAD + GATE_LORA
DECAY_SCALE = math.exp(-0.5)
GN_EPS = 64e-5
LN_EPS = 1e-5
POOL_WINDOWS = (2, 4, 8, 16)
POOL_HALO = 16
PEER_TOPK = 16
N_KEYS = 128
PEER_HALF = 128
NOT_TOP = 99.0
TIME_PAD = 384
VMEM_LIMIT = 56 * 1024 * 1024


def _cparams(sem):
    return pltpu.CompilerParams(dimension_semantics=sem, vmem_limit_bytes=VMEM_LIMIT)


def _pick(n, cands):
    for c in cands:
        if n % c == 0:
            return c
    raise ValueError(f"no tile in {cands} divides {n}")


def _layer_norm(x, g, b):
    mu = jnp.mean(x, axis=-1, keepdims=True)
    xc = x - mu
    var = jnp.mean(xc * xc, axis=-1, keepdims=True)
    return xc * lax.rsqrt(var + LN_EPS) * g + b


def _dot(a, b):
    return jnp.dot(a, b, preferred_element_type=F32)


def _dot_nt(a, b):
    return lax.dot_general(a, b, (((1,), (1,)), ((), ())), preferred_element_type=F32)


def _dot_tn(a, b):
    return lax.dot_general(a, b, (((0,), (0,)), ((), ())), preferred_element_type=F32)


def _ln_inproj_kernel(x_ref, g_ref, b_ref, w_ref, h_ref, p_ref, hb_ref):
    @pl.when(pl.program_id(1) == 0)
    def _():
        h = _layer_norm(x_ref[...], g_ref[...], b_ref[...])
        h_ref[...] = h
        hb_ref[...] = h.astype(BF16)

    p_ref[...] = _dot(hb_ref[...], w_ref[...])


def _ln_inproj(xcat, g, b, w):
    n, d = xcat.shape
    nc = w.shape[1]
    tm = _pick(n, (768, 512, 384, 256, 128))
    tn = _pick(nc, (512, 256, 128))
    return pl.pallas_call(
        _ln_inproj_kernel,
        out_shape=(jax.ShapeDtypeStruct((n, d), F32), jax.ShapeDtypeStruct((n, nc), F32)),
        grid=(n // tm, nc // tn),
        in_specs=[
            pl.BlockSpec((tm, d), lambda i, j: (i, 0)),
            pl.BlockSpec((1, d), lambda i, j: (0, 0)),
            pl.BlockSpec((1, d), lambda i, j: (0, 0)),
            pl.BlockSpec((d, tn), lambda i, j: (0, j)),
        ],
        out_specs=(
            pl.BlockSpec((tm, d), lambda i, j: (i, 0)),
            pl.BlockSpec((tm, tn), lambda i, j: (i, j)),
        ),
        scratch_shapes=[pltpu.VMEM((tm, d), BF16)],
        compiler_params=_cparams(("parallel", "arbitrary")),
        name="ln_inproj",
    )(xcat, g, b, w)


def _token_shift(x, prev_ref, mu):
    rows = x.shape[0]
    rolled = pltpu.roll(x, 1, axis=0)
    first = lax.broadcasted_iota(jnp.int32, x.shape, 0) == 0
    prev = jnp.where(first, prev_ref[0:1, :], rolled)
    prev_ref[0:1, :] = x[rows - 1:rows, :]
    return x + (prev - x) * mu


def _split3(x):
    h1 = x.astype(BF16)
    r1 = x - h1.astype(F32)
    h2 = r1.astype(BF16)
    h3 = (r1 - h2.astype(F32)).astype(BF16)
    return h1, h2, h3


def _seg_sum(x, e):
    hi = x.astype(BF16)
    lo = (x - hi.astype(F32)).astype(BF16)
    return _dot(hi, e) + _dot(lo, e)


def _stack_heads(x, lane_lo):
    return jnp.concatenate([jnp.where(lane_lo, x, 0.0), jnp.where(lane_lo, 0.0, x)], axis=0)


def _pair_chunk(rt, at, kt, bt, v, khat, bhat, wtot, st):
    c = rt.shape[0]
    lane = lax.broadcasted_iota(jnp.int32, (c, LANE), 1)
    row = lax.broadcasted_iota(jnp.int32, (c, LANE), 0)
    lane_lo = lane < RWKV_HEAD
    src = jnp.where(lane_lo, lane, lane - RWKV_HEAD)
    incl = src <= row
    strict = src < row

    lhs = jnp.concatenate([rt, at], axis=0).astype(BF16)
    rhs = jnp.concatenate([_stack_heads(kt, lane_lo), _stack_heads(bt, lane_lo)], axis=0).astype(BF16)
    p = _dot_nt(lhs, rhs)
    ark = jnp.where(incl, p[0:c, 0:LANE], 0.0)
    arb = jnp.where(incl, p[0:c, LANE:2 * LANE], 0.0)
    aak = jnp.where(strict, p[c:2 * c, 0:LANE], 0.0)
    aab = jnp.where(strict, p[c:2 * c, LANE:2 * LANE], 0.0)

    n_bd = _stack_heads(aab, lane_lo)
    r2 = lax.broadcasted_iota(jnp.int32, (2 * c, LANE), 0)
    l2 = lax.broadcasted_iota(jnp.int32, (2 * c, LANE), 1)
    x = jnp.where(r2 == l2, 1.0, 0.0) + n_bd
    nb = n_bd.astype(BF16)
    pw = _dot(nb, nb)
    levels = int(math.log2(c)) - 2
    for _ in range(levels):
        pb = pw.astype(BF16)
        res = _dot(pb, jnp.concatenate([pw, x], axis=1).astype(BF16))
        pw = res[:, 0:LANE]
        x = x + res[:, LANE:2 * LANE]
    x = x + _dot(pw.astype(BF16), x.astype(BF16))
    tb = x.astype(BF16)

    v2 = _stack_heads(v, lane_lo)
    av = _dot(_stack_heads(aak, lane_lo).astype(BF16), v2.astype(BF16))
    res = _dot(tb, jnp.concatenate([av, _stack_heads(at, lane_lo)], axis=1).astype(BF16))
    u0 = res[:, 0:LANE]
    ahat = res[:, LANE:2 * LANE]

    stb = st.astype(BF16)
    x1 = _dot_nt(jnp.concatenate([ahat, rt], axis=0).astype(BF16), stb)
    us = x1[0:2 * c] + u0
    y = x1[2 * c:3 * c] + _dot(jnp.concatenate([ark, arb], axis=1).astype(BF16),
                               jnp.concatenate([v2, us], axis=0).astype(BF16))
    u = us[0:c] + us[c:2 * c]
    upd = _dot_tn(jnp.concatenate([v, u], axis=0).astype(BF16),
                  jnp.concatenate([khat, bhat], axis=0).astype(BF16))
    same_head = (r2 < RWKV_HEAD) == (l2 < RWKV_HEAD)
    st_new = st * wtot + jnp.where(same_head, upd, 0.0)
    return y, st_new


def _rwkv_kernel(pr_ref, pk_ref, pv_ref, pl_ref, mur_ref, muk_ref, muv_ref, mul_ref,
                 ww_ref, wa_ref, wg_ref, par_ref, e_ref, ltri_ref,
                 ya_ref,
                 prev_r, prev_k, prev_v, prev_l, st_ref, y_scr):
    tb, hgw = pr_ref.shape
    n_pairs = hgw // LANE
    n_chunks = tb // CHUNK

    @pl.when(pl.program_id(2) == 0)
    def _():
        prev_r[...] = jnp.zeros_like(prev_r)
        prev_k[...] = jnp.zeros_like(prev_k)
        prev_v[...] = jnp.zeros_like(prev_v)
        prev_l[...] = jnp.zeros_like(prev_l)
        st_ref[...] = jnp.zeros_like(st_ref)

    r = _token_shift(pr_ref[...], prev_r, mur_ref[...])
    k = _token_shift(pk_ref[...], prev_k, muk_ref[...])
    v = _token_shift(pv_ref[...], prev_v, muv_ref[...])
    lo = _token_shift(pl_ref[...], prev_l, mul_ref[...])

    w0 = par_ref[0:1, :]
    a0 = par_ref[1:2, :]
    k_k = par_ref[2:3, :]
    k_a = par_ref[3:4, :]
    r_k = par_ref[4:5, :]
    gn_g = par_ref[5:6, :]
    gn_b = par_ref[6:7, :]
    e = e_ref[...]

    wl = jnp.tanh(lo[:, 0:LORA_PAD]).astype(BF16)
    al = lo[:, LORA_PAD:2 * LORA_PAD].astype(BF16)
    gl = jax.nn.sigmoid(lo[:, 2 * LORA_PAD:LORA_W]).astype(BF16)
    logw = -DECAY_SCALE * jax.nn.sigmoid(w0 + _dot(wl, ww_ref[...]))
    a = jax.nn.sigmoid(a0 + _dot(al, wa_ref[...]))
    g = _dot(gl, wg_ref[...])

    kkr = k * k_k
    kk = kkr / jnp.maximum(jnp.sqrt(_seg_sum(kkr * kkr, e)), 1e-12)
    k2 = k * (1.0 + (a - 1.0) * k_a)
    bonus = _seg_sum(r * k2 * r_k, e) * v
    beta = kk * a

    ltri = ltri_ref[...]
    h1, h2, h3 = _split3(logw)
    cum = _dot(ltri, h1) + _dot(ltri, h2) + _dot(ltri, h3)

    rt_all = r * jnp.exp(cum)
    inv = jnp.exp(-cum)
    kt_all = k2 * inv
    bt_all = beta * inv
    at_all = -kk * jnp.exp(cum - logw)

    for ci in range(n_chunks):
        rows = slice(ci * CHUNK, (ci + 1) * CHUNK)
        tot = cum[(ci + 1) * CHUNK - 1:(ci + 1) * CHUNK, :]
        tail = jnp.exp(tot - cum[rows])
        khat_c = k2[rows] * tail
        bhat_c = beta[rows] * tail
        wtot_c = jnp.exp(tot)
        for q in range(n_pairs):
            ln = slice(q * LANE, (q + 1) * LANE)
            y, st_new = _pair_chunk(rt_all[rows, ln], at_all[rows, ln], kt_all[rows, ln],
                                    bt_all[rows, ln], v[rows, ln], khat_c[:, ln], bhat_c[:, ln],
                                    wtot_c[:, ln], st_ref[q])
            st_ref[q] = st_new
            y_scr[rows, ln] = y

    y = y_scr[...]
    inv_n = 1.0 / RWKV_HEAD
    ym = _seg_sum(y, e) * inv_n
    yc = y - ym
    yv = _seg_sum(yc * yc, e) * inv_n
    yn = yc * lax.rsqrt(yv + GN_EPS) * gn_g + gn_b
    ya_ref[...] = ((yn + bonus) * g).astype(BF16)


def _rwkv(p, mu, ww, wa, wg, par, batch, tp, d, lora_off):
    n = p.shape[0]
    hgw = 512 if d % 512 == 0 else d
    tb = _pick(tp, (128, 64))
    ntb = tp // tb
    groups = d // hgw
    koff, voff = d // hgw, 2 * d // hgw
    loff = lora_off // LORA_W
    assert lora_off % LORA_W == 0

    head_id = jnp.arange(hgw) // RWKV_HEAD
    e = (head_id[:, None] == head_id[None, :]).astype(BF16)
    t_id = jnp.arange(tb)
    ltri = ((t_id[:, None] >= t_id[None, :]) &
            (t_id[:, None] // CHUNK == t_id[None, :] // CHUNK)).astype(BF16)

    row = lambda b, g, t: b * ntb + t
    return pl.pallas_call(
        _rwkv_kernel,
        out_shape=jax.ShapeDtypeStruct((n, d), BF16),
        grid=(batch, groups, ntb),
        in_specs=[
            pl.BlockSpec((tb, hgw), lambda b, g, t: (row(b, g, t), g)),
            pl.BlockSpec((tb, hgw), lambda b, g, t: (row(b, g, t), koff + g)),
            pl.BlockSpec((tb, hgw), lambda b, g, t: (row(b, g, t), voff + g)),
            pl.BlockSpec((tb, LORA_W), lambda b, g, t: (row(b, g, t), loff)),
            pl.BlockSpec((1, hgw), lambda b, g, t: (0, g)),
            pl.BlockSpec((1, hgw), lambda b, g, t: (0, koff + g)),
            pl.BlockSpec((1, hgw), lambda b, g, t: (0, voff + g)),
            pl.BlockSpec((1, LORA_W), lambda b, g, t: (0, loff)),
            pl.BlockSpec((LORA_PAD, hgw), lambda b, g, t: (0, g)),
            pl.BlockSpec((LORA_PAD, hgw), lambda b, g, t: (0, g)),
            pl.BlockSpec((GATE_LORA, hgw), lambda b, g, t: (0, g)),
            pl.BlockSpec((8, hgw), lambda b, g, t: (0, g)),
            pl.BlockSpec((hgw, hgw), lambda b, g, t: (0, 0)),
            pl.BlockSpec((tb, tb), lambda b, g, t: (0, 0)),
        ],
        out_specs=pl.BlockSpec((tb, hgw), lambda b, g, t: (row(b, g, t), g)),
        scratch_shapes=[
            pltpu.VMEM((8, hgw), F32), pltpu.VMEM((8, hgw), F32), pltpu.VMEM((8, hgw), F32),
            pltpu.VMEM((8, LORA_W), F32),
            pltpu.VMEM((hgw // LANE, LANE, LANE), F32),
            pltpu.VMEM((tb, hgw), F32),
        ],
        compiler_params=_cparams(("parallel", "parallel", "arbitrary")),
        name="rwkv",
    )(p, p, p, p, mu, mu, mu, mu, ww, wa, wg, par, e, ltri)


def _pool_kernel(u_ref, w_ref, s_ref, yb_ref, tail_ref):
    tb, pd = u_ref.shape
    gd = pd // len(POOL_WINDOWS)
    t = pl.program_id(1)

    @pl.when(t == 0)
    def _():
        tail_ref[...] = jnp.zeros_like(tail_ref)

    u = u_ref[...]
    ext = jnp.concatenate([tail_ref[...], u], axis=0)
    tail_ref[...] = u[tb - POOL_HALO:tb, :]
    pos = t * tb + lax.broadcasted_iota(jnp.int32, (tb, 1), 0) + 1
    for gi, win in enumerate(POOL_WINDOWS):
        ln = slice(gi * gd, (gi + 1) * gd)
        s = ext[:, ln]
        step = 1
        while step < win:
            s = s + pltpu.roll(s, step, axis=0)
            step *= 2
        cnt = jnp.minimum(pos, win).astype(F32)
        pooled = s[POOL_HALO:, :] / cnt - u[:, ln]
        mixed = _dot(pooled.astype(BF16), w_ref[gi])
        yb_ref[:, ln] = (mixed * s_ref[:, ln]).astype(BF16)


def _pool(p, w_pool, scale, batch, tp, pool_off):
    n = p.shape[0]
    ng, gd, _ = w_pool.shape
    pd = ng * gd
    assert ng == len(POOL_WINDOWS) and pool_off % pd == 0 and gd % LANE == 0
    assert all(w & (w - 1) == 0 and w <= POOL_HALO for w in POOL_WINDOWS)
    tb = _pick(tp, (384, 256, 128))
    ntb = tp // tb
    return pl.pallas_call(
        _pool_kernel,
        out_shape=jax.ShapeDtypeStruct((n, pd), BF16),
        grid=(batch, ntb),
        in_specs=[
            pl.BlockSpec((tb, pd), lambda b, t: (b * ntb + t, pool_off // pd)),
            pl.BlockSpec((ng, gd, gd), lambda b, t: (0, 0, 0)),
            pl.BlockSpec((1, pd), lambda b, t: (0, 0)),
        ],
        out_specs=pl.BlockSpec((tb, pd), lambda b, t: (b * ntb + t, 0)),
        scratch_shapes=[pltpu.VMEM((POOL_HALO, pd), F32)],
        compiler_params=_cparams(("parallel", "arbitrary")),
        name="pool",
    )(p, w_pool, scale)


def _merge_kernel(ya_ref, yb_ref, wa_ref, wb_ref, g0_ref, g1_ref, bg_ref, o_ref):
    g0 = jax.nn.sigmoid(g0_ref[...] + bg_ref[0:1, :])
    g1 = jax.nn.sigmoid(g1_ref[...] + bg_ref[1:2, :])
    o_ref[...] = (g0 * _dot(ya_ref[...], wa_ref[...]) + g1 * _dot(yb_ref[...], wb_ref[...])).astype(BF16)


def _merge(ya, yb, wa, wb, p, b_gate, gate_off):
    n, d = ya.shape
    pd = yb.shape[1]
    tm = _pick(n, (768, 512, 384, 256, 128))
    tn = _pick(d, (512, 256, 128))
    assert gate_off % tn == 0
    g0 = gate_off // tn
    g1 = (gate_off + d) // tn
    return pl.pallas_call(
        _merge_kernel,
        out_shape=jax.ShapeDtypeStruct((n, d), BF16),
        grid=(n // tm, d // tn),
        in_specs=[
            pl.BlockSpec((tm, d), lambda i, j: (i, 0)),
            pl.BlockSpec((tm, pd), lambda i, j: (i, 0)),
            pl.BlockSpec((d, tn), lambda i, j: (0, j)),
            pl.BlockSpec((pd, tn), lambda i, j: (0, j)),
            pl.BlockSpec((tm, tn), lambda i, j: (i, g0 + j)),
            pl.BlockSpec((tm, tn), lambda i, j: (i, g1 + j)),
            pl.BlockSpec((2, tn), lambda i, j: (0, j)),
        ],
        out_specs=pl.BlockSpec((tm, tn), lambda i, j: (i, j)),
        compiler_params=_cparams(("parallel", "arbitrary")),
        name="merge",
    )(ya, yb, wa, wb, p, p, b_gate)


def _outproj_kernel(alpha, m_ref, w_ref, h_ref, g_ref, b_ref, o_ref, ot_ref):
    y = alpha * h_ref[...] + _dot(m_ref[...], w_ref[...])
    h1 = _layer_norm(y, g_ref[...], b_ref[...])
    o_ref[...] = h1
    ot_ref[...] = h1.T.astype(BF16)


def _outproj_ln(merged, w_out, h0, g, b, alpha):
    n, d = merged.shape
    tm = _pick(n, (256, 128))
    return pl.pallas_call(
        functools.partial(_outproj_kernel, alpha),
        out_shape=(jax.ShapeDtypeStruct((n, d), F32), jax.ShapeDtypeStruct((d, n), BF16)),
        grid=(n // tm,),
        in_specs=[
            pl.BlockSpec((tm, d), lambda i: (i, 0)),
            pl.BlockSpec((d, d), lambda i: (0, 0)),
            pl.BlockSpec((tm, d), lambda i: (i, 0)),
            pl.BlockSpec((1, d), lambda i: (0, 0)),
            pl.BlockSpec((1, d), lambda i: (0, 0)),
        ],
        out_specs=(
            pl.BlockSpec((tm, d), lambda i: (i, 0)),
            pl.BlockSpec((d, tm), lambda i: (0, i)),
        ),
        compiler_params=_cparams(("parallel",)),
        name="outproj_ln",
    )(merged, w_out, h0, g, b)


def _top_k_ranks(s):
    key = lax.broadcasted_iota(jnp.int32, s.shape, 0).astype(F32)
    work = s
    rank = jnp.full(s.shape, NOT_TOP, F32)
    vals = []
    for r in range(PEER_TOPK):
        m = jnp.max(work, axis=0, keepdims=True)
        first = jnp.min(jnp.where(work == m, key, float(s.shape[0])), axis=0, keepdims=True)
        sel = key == first
        rank = jnp.where(sel, float(r + 1), rank)
        work = jnp.where(sel, -jnp.inf, work)
        vals.append(m)
    return rank, jnp.concatenate(vals, axis=0)


def _candidate_rows():
    pairs = [(0, j) for j in range(PEER_TOPK)]
    half = PEER_TOPK // 2
    for i in range(1, half):
        pairs += [(i, j) for j in range(half)]
    pairs += [(i, 0) for i in range(half, PEER_TOPK)]
    return pairs


def _route_kernel(h_ref, wq_ref, keys_ref, pos_ref, a_ref, c_ref, b_ref, r_ref, q_ref):
    tn = h_ref.shape[1]
    n_heads = a_ref.shape[0]
    half = PEER_TOPK // 2
    q_ref[...] = _dot(wq_ref[...], h_ref[...]).astype(BF16)
    pos = pos_ref[...]
    n_rows = pos.shape[0]
    slot = lax.broadcasted_iota(jnp.int32, (PEER_TOPK, tn), 0).astype(F32)

    def head(h, carry):
        base = pl.multiple_of(h * 2 * PEER_HALF, 2 * PEER_HALF)
        s1 = _dot(keys_ref[0], q_ref[pl.ds(base, PEER_HALF), :])
        s2 = _dot(keys_ref[1], q_ref[pl.ds(base + PEER_HALF, PEER_HALF), :])
        rank1, sv1 = _top_k_ranks(s1)
        rank2, sv2 = _top_k_ranks(s2)

        blocks = [sv1[0:1] + sv2]
        for i in range(1, half):
            blocks.append(sv1[i:i + 1] + sv2[0:half])
        blocks.append(sv1[half:PEER_TOPK] + sv2[0:1])
        cand = jnp.concatenate(blocks, axis=0)
        posb = jnp.broadcast_to(pos, (n_rows, tn))
        cnt = jnp.zeros((PEER_TOPK, tn), F32)
        z = jnp.ones((1, tn), F32)
        smax = None
        for it in range(PEER_TOPK):
            m = jnp.max(cand, axis=0, keepdims=True)
            first = jnp.min(jnp.where(cand == m, posb, float(PEER_TOPK * PEER_TOPK)), axis=0, keepdims=True)
            cand = jnp.where(posb == first, -jnp.inf, cand)
            cnt = cnt + jnp.where(slot == jnp.floor(first * (1.0 / PEER_TOPK)), 1.0, 0.0)
            if it == 0:
                smax = m
            else:
                z = z + jnp.exp(m - smax)

        ci = jnp.zeros(s1.shape, F32)
        for r in range(PEER_TOPK):
            ci = ci + jnp.where(rank1 == float(r + 1), cnt[r:r + 1], 0.0)
        a_ref[h] = jnp.where(rank1 <= PEER_TOPK, jnp.exp(s1 - sv1[0:1]), 0.0)
        c_ref[h] = ci
        b_ref[h] = jnp.where(rank2 <= PEER_TOPK, jnp.exp(s2 - sv2[0:1]) / z, 0.0)
        r_ref[h] = rank2
        return carry

    lax.fori_loop(0, n_heads, head, 0)


def _peer_route(h1t, wq_t, keys):
    d, n = h1t.shape
    hq = wq_t.shape[0]
    n_heads = hq // (2 * PEER_HALF)
    tn = _pick(n, (256, 128))
    pairs = _candidate_rows()
    pos = jnp.asarray([[i * PEER_TOPK + j] for i, j in pairs], F32)
    shp = jax.ShapeDtypeStruct((n_heads, N_KEYS, n), F32)
    blk = pl.BlockSpec((n_heads, N_KEYS, tn), lambda i: (0, 0, i))
    return pl.pallas_call(
        _route_kernel,
        out_shape=(shp, shp, shp, shp),
        grid=(n // tn,),
        in_specs=[
            pl.BlockSpec((d, tn), lambda i: (0, i)),
            pl.BlockSpec((hq, d), lambda i: (0, 0)),
            pl.BlockSpec((2, N_KEYS, PEER_HALF), lambda i: (0, 0, 0)),
            pl.BlockSpec((len(pairs), 1), lambda i: (0, 0)),
        ],
        out_specs=(blk, blk, blk, blk),
        scratch_shapes=[pltpu.VMEM((hq, tn), BF16)],
        compiler_params=_cparams(("parallel",)),
        name="peer_route",
    )(h1t, wq_t, keys, pos)


def _gelu(x):
    return 0.5 * x * (1.0 + lax.erf(x * (1.0 / math.sqrt(2.0))))


def _dense_kernel(alpha, ht_ref, u_ref, v_ref, a_ref, c_ref, b_ref, r_ref, h1_ref, g_ref, bb_ref,
                  o_ref, acc_ref):
    k = pl.program_id(1)
    te = u_ref.shape[0]
    n_heads = a_ref.shape[0]
    sub = te // N_KEYS

    @pl.when(k == 0)
    def _():
        acc_ref[...] = jnp.zeros_like(acc_ref)

    sc = _dot(u_ref[...], ht_ref[...])
    acts = []
    for ii in range(sub):
        i = k * sub + ii
        w = None
        for h in range(n_heads):
            a_row = a_ref[h, pl.ds(i, 1), :]
            c_row = c_ref[h, pl.ds(i, 1), :]
            term = a_row * jnp.where(r_ref[h] <= c_row, b_ref[h], 0.0)
            w = term if w is None else w + term
        acts.append((_gelu(sc[ii * N_KEYS:(ii + 1) * N_KEYS]) * w).astype(BF16))
    act = jnp.concatenate(acts, axis=0)
    acc_ref[...] += _dot_tn(act, v_ref[...])

    @pl.when(k == pl.num_programs(1) - 1)
    def _():
        o_ref[...] = _layer_norm(alpha * h1_ref[...] + acc_ref[...], g_ref[...], bb_ref[...])


def _peer_dense(h1t, u, v, a, c, b, r, h1, g, bb, alpha):
    d, n = h1t.shape
    ne = u.shape[0]
    n_heads = a.shape[0]
    tn = _pick(n, (384, 256, 128))
    te = _pick(ne, (512, 256, 128))
    rblk = pl.BlockSpec((n_heads, N_KEYS, tn), lambda i, k: (0, 0, i))
    return pl.pallas_call(
        functools.partial(_dense_kernel, alpha),
        out_shape=jax.ShapeDtypeStruct((n, d), F32),
        grid=(n // tn, ne // te),
        in_specs=[
            pl.BlockSpec((d, tn), lambda i, k: (0, i)),
            pl.BlockSpec((te, d), lambda i, k: (k, 0)),
            pl.BlockSpec((te, d), lambda i, k: (k, 0)),
            rblk, rblk, rblk, rblk,
            pl.BlockSpec((tn, d), lambda i, k: (i, 0)),
            pl.BlockSpec((1, d), lambda i, k: (0, 0)),
            pl.BlockSpec((1, d), lambda i, k: (0, 0)),
        ],
        out_specs=pl.BlockSpec((tn, d), lambda i, k: (i, 0)),
        scratch_shapes=[pltpu.VMEM((tn, d), F32)],
        compiler_params=_cparams(("parallel", "arbitrary")),
        name="peer_dense",
    )(h1t, u, v, a, c, b, r, h1, g, bb)


def _pad_rows(w, rows):
    return jnp.pad(w, ((0, rows - w.shape[0]), (0, 0)))


def kernel(x, meta_tokens, ln_in_g, ln_in_b, w_in, mu_shift, w0, w_w2, a0, w_a2, w_g2, k_k, k_a, r_k, gn_g, gn_b, w_pool, pool_scale, b_gate, w_branch_a, w_branch_b, w_out, ln1_g, ln1_b, w_q_peer, sub_keys, u_table, v_table, ln2_g, ln2_b):
    batch, seq, d = x.shape
    depth = w_in.shape[0]
    n_meta = meta_tokens.shape[0]
    t = n_meta + seq
    tp = -(-t // TIME_PAD) * TIME_PAD
    n = batch * tp
    pd = w_pool.shape[1] * w_pool.shape[2]
    assert d % LANE == 0 and w_w2.shape[1] == DECAY_LORA and w_a2.shape[1] == ICL_LORA
    assert w_g2.shape[1] == GATE_LORA and sub_keys.shape[2] == N_KEYS and sub_keys.shape[3] == PEER_HALF
    alpha = (2.0 * depth) ** 0.25

    meta = jnp.broadcast_to(meta_tokens[None].astype(x.dtype), (batch, n_meta, d))
    hcat = jnp.concatenate([meta, x, jnp.zeros((batch, tp - t, d), x.dtype)], axis=1).reshape(n, d)

    c_rkv = 3 * d
    c_wl = c_rkv
    c_al = c_wl + DECAY_LORA
    c_gl = c_al + ICL_LORA
    c_pool = c_gl + GATE_LORA
    c_gate = c_pool + pd
    pool_off = 3 * d
    gate_off = pool_off + pd
    lora_off = gate_off + 2 * d

    def relayout(w):
        z = jnp.zeros(w.shape[:-1] + (LORA_PAD - DECAY_LORA,), w.dtype)
        z2 = jnp.zeros(w.shape[:-1] + (LORA_PAD - ICL_LORA,), w.dtype)
        return jnp.concatenate([w[..., :c_rkv], w[..., c_pool:c_gate], w[..., c_gate:],
                                w[..., c_wl:c_al], z, w[..., c_al:c_gl], z2, w[..., c_gl:c_pool]], axis=-1)

    h = None
    for l in range(depth):
        w_in_p = relayout(w_in[l]).astype(BF16)
        mu_p = jnp.concatenate([mu_shift[l], jnp.zeros((w_in.shape[2] - mu_shift.shape[1],), F32)])
        mu_p = relayout(mu_p[None])
        if l == 0:
            h0, p = _ln_inproj(hcat, ln_in_g[None], ln_in_b[None], w_in_p)
        else:
            raise NotImplementedError("depth > 1")
        par = jnp.stack([w0[l], a0[l], k_k[l], k_a[l], r_k[l].reshape(-1), gn_g[l], gn_b[l],
                         jnp.zeros((d,), F32)])
        ya = _rwkv(p, mu_p, _pad_rows(w_w2[l], LORA_PAD).astype(BF16), _pad_rows(w_a2[l], LORA_PAD).astype(BF16),
                   w_g2[l].astype(BF16), par, batch, tp, d, lora_off)
        yb = _pool(p, w_pool[l].astype(BF16), pool_scale[l][None], batch, tp, pool_off)
        merged = _merge(ya, yb, w_branch_a[l].astype(BF16), w_branch_b[l].astype(BF16), p, b_gate[l], gate_off)
        h1, h1t = _outproj_ln(merged, w_out[l].astype(BF16), h0, ln1_g[l][None], ln1_b[l][None], alpha)
        ra, rc, rb, rr = _peer_route(h1t, w_q_peer[l].T.astype(BF16), sub_keys[l].astype(BF16))
        h = _peer_dense(h1t, u_table[l].astype(BF16), v_table[l].astype(BF16), ra, rc, rb, rr,
                        h1, ln2_g[l][None], ln2_b[l][None], alpha)
    return h.reshape(batch, tp, d)[:, n_meta:t]
```

```python
import functools
import math

import jax
import jax.numpy as jnp
from jax import lax
from jax.experimental import pallas as pl
from jax.experimental.pallas import tpu as pltpu

F32 = jnp.float32
BF16 = jnp.bfloat16

LANE = 128
N_META = 16
RWKV_HEAD = 64
CHUNK = 64
DECAY_LORA = 96
ICL_LORA = 96
GATE_LORA = 256
LORA_PAD = 128
LORA_W = 2 * LORA_PAD + GATE_LORA
DECAY_SCALE = math.exp(-0.5)
GN_EPS = 64e-5
LN_EPS = 1e-5
POOL_WINDOWS = (2, 4, 8, 16)
POOL_HALO = 16
PEER_TOPK = 16
N_KEYS = 128
PEER_HALF = 128
NOT_TOP = 99.0
TIME_PAD = 384
VMEM_LIMIT = 56 * 1024 * 1024


def _cparams(sem):
    return pltpu.CompilerParams(dimension_semantics=sem, vmem_limit_bytes=VMEM_LIMIT)


def _pick(n, cands):
    for c in cands:
        if n % c == 0:
            return c
    raise ValueError(f"no tile in {cands} divides {n}")


def _layer_norm(x, g, b):
    mu = jnp.mean(x, axis=-1, keepdims=True)
    xc = x - mu
    var = jnp.mean(xc * xc, axis=-1, keepdims=True)
    return xc * lax.rsqrt(var + LN_EPS) * g + b


def _dot(a, b):
    return jnp.dot(a, b, preferred_element_type=F32)


def _dot_nt(a, b):
    return lax.dot_general(a, b, (((1,), (1,)), ((), ())), preferred_element_type=F32)


def _dot_tn(a, b):
    return lax.dot_general(a, b, (((0,), (0,)), ((), ())), preferred_element_type=F32)


def _ln_inproj_kernel(x_ref, g_ref, b_ref, w_ref, h_ref, p_ref, hb_ref):
    @pl.when(pl.program_id(1) == 0)
    def _():
        h = _layer_norm(x_ref[...], g_ref[...], b_ref[...])
        h_ref[...] = h
        hb_ref[...] = h.astype(BF16)

    p_ref[...] = _dot(hb_ref[...], w_ref[...])


def _ln_inproj(xcat, g, b, w):
    n, d = xcat.shape
    nc = w.shape[1]
    tm = _pick(n, (768, 512, 384, 256, 128))
    tn = _pick(nc, (512, 256, 128))
    return pl.pallas_call(
        _ln_inproj_kernel,
        out_shape=(jax.ShapeDtypeStruct((n, d), F32), jax.ShapeDtypeStruct((n, nc), F32)),
        grid=(n // tm, nc // tn),
        in_specs=[
            pl.BlockSpec((tm, d), lambda i, j: (i, 0)),
            pl.BlockSpec((1, d), lambda i, j: (0, 0)),
            pl.BlockSpec((1, d), lambda i, j: (0, 0)),
            pl.BlockSpec((d, tn), lambda i, j: (0, j)),
        ],
        out_specs=(
            pl.BlockSpec((tm, d), lambda i, j: (i, 0)),
            pl.BlockSpec((tm, tn), lambda i, j: (i, j)),
        ),
        scratch_shapes=[pltpu.VMEM((tm, d), BF16)],
        compiler_params=_cparams(("parallel", "arbitrary")),
        name="ln_inproj",
    )(xcat, g, b, w)


def _token_shift(x, prev_ref, mu):
    rows = x.shape[0]
    rolled = pltpu.roll(x, 1, axis=0)
    first = lax.broadcasted_iota(jnp.int32, x.shape, 0) == 0
    prev = jnp.where(first, prev_ref[0:1, :], rolled)
    prev_ref[0:1, :] = x[rows - 1:rows, :]
    return x + (prev - x) * mu


def _split3(x):
    h1 = x.astype(BF16)
    r1 = x - h1.astype(F32)
    h2 = r1.astype(BF16)
    h3 = (r1 - h2.astype(F32)).astype(BF16)
    return h1, h2, h3


def _seg_sum(x, e):
    hi = x.astype(BF16)
    lo = (x - hi.astype(F32)).astype(BF16)
    return _dot(hi, e) + _dot(lo, e)


def _stack_heads(x, lane_lo):
    return jnp.concatenate([jnp.where(lane_lo, x, 0.0), jnp.where(lane_lo, 0.0, x)], axis=0)


def _pair_masks(c):
    lane = lax.broadcasted_iota(jnp.int32, (c, LANE), 1)
    row = lax.broadcasted_iota(jnp.int32, (c, LANE), 0)
    lane_lo = lane < RWKV_HEAD
    src = jnp.where(lane_lo, lane, lane - RWKV_HEAD)
    r2 = lax.broadcasted_iota(jnp.int32, (2 * c, LANE), 0)
    l2 = lax.broadcasted_iota(jnp.int32, (2 * c, LANE), 1)
    return lane_lo, src <= row, src < row, r2 == l2, (r2 < RWKV_HEAD) == (l2 < RWKV_HEAD)


def _chunk_prepare(rt, at, kt, bt, v):
    c = rt[0].shape[0]
    lane_lo, incl, strict, eye, _ = _pair_masks(c)
    stack = lambda x: _stack_heads(x, lane_lo)
    n = range(len(rt))

    lhs = [jnp.concatenate([rt[i], at[i]], axis=0).astype(BF16) for i in n]
    rhs = [jnp.concatenate([stack(kt[i]), stack(bt[i])], axis=0).astype(BF16) for i in n]
    p = [_dot_nt(lhs[i], rhs[i]) for i in n]
    arkb = [jnp.concatenate([jnp.where(incl, p[i][0:c, 0:LANE], 0.0),
                             jnp.where(incl, p[i][0:c, LANE:2 * LANE], 0.0)], axis=1).astype(BF16) for i in n]
    aak = [stack(jnp.where(strict, p[i][c:2 * c, 0:LANE], 0.0)).astype(BF16) for i in n]
    n_bd = [stack(jnp.where(strict, p[i][c:2 * c, LANE:2 * LANE], 0.0)) for i in n]

    x = [jnp.where(eye, 1.0, 0.0) + n_bd[i] for i in n]
    nb = [n_bd[i].astype(BF16) for i in n]
    pw = [_dot(nb[i], nb[i]) for i in n]
    for _ in range(int(math.log2(c)) - 2):
        res = [_dot(pw[i].astype(BF16), jnp.concatenate([pw[i], x[i]], axis=1).astype(BF16)) for i in n]
        pw = [res[i][:, 0:LANE] for i in n]
        x = [x[i] + res[i][:, LANE:2 * LANE] for i in n]
    x = [x[i] + _dot(pw[i].astype(BF16), x[i].astype(BF16)) for i in n]

    v2 = [stack(v[i]).astype(BF16) for i in n]
    av = [_dot(aak[i], v2[i]) for i in n]
    res = [_dot(x[i].astype(BF16), jnp.concatenate([av[i], stack(at[i])], axis=1).astype(BF16)) for i in n]
    u0 = [res[i][:, 0:LANE] for i in n]
    ahat = [res[i][:, LANE:2 * LANE] for i in n]
    return arkb, v2, u0, ahat


def _chunk_advance(st, rt, v, khat, bhat, wtot, arkb, v2, u0, ahat):
    c = rt[0].shape[0]
    same_head = _pair_masks(c)[4]
    n = range(len(rt))
    x1 = [_dot_nt(jnp.concatenate([ahat[i], rt[i]], axis=0).astype(BF16), st[i].astype(BF16)) for i in n]
    us = [x1[i][0:2 * c] + u0[i] for i in n]
    y = [x1[i][2 * c:3 * c] + _dot(arkb[i], jnp.concatenate([v2[i], us[i].astype(BF16)], axis=0)) for i in n]
    u = [us[i][0:c] + us[i][c:2 * c] for i in n]
    upd = [_dot_tn(jnp.concatenate([v[i], u[i]], axis=0).astype(BF16),
                   jnp.concatenate([khat[i], bhat[i]], axis=0).astype(BF16)) for i in n]
    st_new = [st[i] * wtot[i] + jnp.where(same_head, upd[i], 0.0) for i in n]
    return y, st_new


def _rwkv_kernel(pr_ref, pk_ref, pv_ref, pl_ref, mur_ref, muk_ref, muv_ref, mul_ref,
                 ww_ref, wa_ref, wg_ref, par_ref, e_ref, ltri_ref,
                 ya_ref,
                 prev_r, prev_k, prev_v, prev_l, st_ref):
    tb, hgw = pr_ref.shape
    n_pairs = hgw // LANE
    n_chunks = tb // CHUNK

    @pl.when(pl.program_id(2) == 0)
    def _():
        prev_r[...] = jnp.zeros_like(prev_r)
        prev_k[...] = jnp.zeros_like(prev_k)
        prev_v[...] = jnp.zeros_like(prev_v)
        prev_l[...] = jnp.zeros_like(prev_l)
        st_ref[...] = jnp.zeros_like(st_ref)

    r = _token_shift(pr_ref[...], prev_r, mur_ref[...])
    k = _token_shift(pk_ref[...], prev_k, muk_ref[...])
    v = _token_shift(pv_ref[...], prev_v, muv_ref[...])
    lo = _token_shift(pl_ref[...], prev_l, mul_ref[...])

    w0 = par_ref[0:1, :]
    a0 = par_ref[1:2, :]
    k_k = par_ref[2:3, :]
    k_a = par_ref[3:4, :]
    r_k = par_ref[4:5, :]
    gn_g = par_ref[5:6, :]
    gn_b = par_ref[6:7, :]
    e = e_ref[...]

    wl = jnp.tanh(lo[:, 0:LORA_PAD]).astype(BF16)
    al = lo[:, LORA_PAD:2 * LORA_PAD].astype(BF16)
    gl = jax.nn.sigmoid(lo[:, 2 * LORA_PAD:LORA_W]).astype(BF16)
    logw = -DECAY_SCALE * jax.nn.sigmoid(w0 + _dot(wl, ww_ref[...]))
    a = jax.nn.sigmoid(a0 + _dot(al, wa_ref[...]))
    g = _dot(gl, wg_ref[...])

    kkr = k * k_k
    kk = kkr / jnp.maximum(jnp.sqrt(_seg_sum(kkr * kkr, e)), 1e-12)
    k2 = k * (1.0 + (a - 1.0) * k_a)
    bonus = _seg_sum(r * k2 * r_k, e) * v
    beta = kk * a

    ltri = ltri_ref[...]
    h1, h2, h3 = _split3(logw)
    cum = _dot(ltri, h1) + _dot(ltri, h2) + _dot(ltri, h3)

    rt_all = r * jnp.exp(cum)
    inv = jnp.exp(-cum)
    kt_all = k2 * inv
    bt_all = beta * inv
    at_all = -kk * jnp.exp(cum - logw)

    tiles = lambda z: [z[ci * CHUNK:(ci + 1) * CHUNK, q * LANE:(q + 1) * LANE]
                       for ci in range(n_chunks) for q in range(n_pairs)]
    rt_t, v_t = tiles(rt_all), tiles(v)
    arkb, v2, u0, ahat = _chunk_prepare(rt_t, tiles(at_all), tiles(kt_all), tiles(bt_all), v_t)

    st = [st_ref[q] for q in range(n_pairs)]
    y_rows = []
    for ci in range(n_chunks):
        rows = slice(ci * CHUNK, (ci + 1) * CHUNK)
        tot = cum[(ci + 1) * CHUNK - 1:(ci + 1) * CHUNK, :]
        tail = jnp.exp(tot - cum[rows])
        khat_c = k2[rows] * tail
        bhat_c = beta[rows] * tail
        wtot_c = jnp.exp(tot)
        pair = lambda z: [z[:, q * LANE:(q + 1) * LANE] for q in range(n_pairs)]
        sel = slice(ci * n_pairs, (ci + 1) * n_pairs)
        y_c, st = _chunk_advance(st, rt_t[sel], v_t[sel], pair(khat_c), pair(bhat_c), pair(wtot_c),
                                 arkb[sel], v2[sel], u0[sel], ahat[sel])
        y_rows.append(jnp.concatenate(y_c, axis=1))
    for q in range(n_pairs):
        st_ref[q] = st[q]

    y = jnp.concatenate(y_rows, axis=0)
    inv_n = 1.0 / RWKV_HEAD
    ym = _seg_sum(y, e) * inv_n
    yc = y - ym
    yv = _seg_sum(yc * yc, e) * inv_n
    yn = yc * lax.rsqrt(yv + GN_EPS) * gn_g + gn_b
    ya_ref[...] = ((yn + bonus) * g).astype(BF16)


def _rwkv(p, mu, ww, wa, wg, par, batch, tp, d, lora_off):
    n = p.shape[0]
    hgw = 512 if d % 512 == 0 else d
    tb = _pick(tp, (128, 64))
    ntb = tp // tb
    groups = d // hgw
    koff, voff = d // hgw, 2 * d // hgw
    loff = lora_off // LORA_W
    assert lora_off % LORA_W == 0

    head_id = jnp.arange(hgw) // RWKV_HEAD
    e = (head_id[:, None] == head_id[None, :]).astype(BF16)
    t_id = jnp.arange(tb)
    ltri = ((t_id[:, None] >= t_id[None, :]) &
            (t_id[:, None] // CHUNK == t_id[None, :] // CHUNK)).astype(BF16)

    row = lambda b, g, t: b * ntb + t
    return pl.pallas_call(
        _rwkv_kernel,
        out_shape=jax.ShapeDtypeStruct((n, d), BF16),
        grid=(batch, groups, ntb),
        in_specs=[
            pl.BlockSpec((tb, hgw), lambda b, g, t: (row(b, g, t), g)),
            pl.BlockSpec((tb, hgw), lambda b, g, t: (row(b, g, t), koff + g)),
            pl.BlockSpec((tb, hgw), lambda b, g, t: (row(b, g, t), voff + g)),
            pl.BlockSpec((tb, LORA_W), lambda b, g, t: (row(b, g, t), loff)),
            pl.BlockSpec((1, hgw), lambda b, g, t: (0, g)),
            pl.BlockSpec((1, hgw), lambda b, g, t: (0, koff + g)),
            pl.BlockSpec((1, hgw), lambda b, g, t: (0, voff + g)),
            pl.BlockSpec((1, LORA_W), lambda b, g, t: (0, loff)),
            pl.BlockSpec((LORA_PAD, hgw), lambda b, g, t: (0, g)),
            pl.BlockSpec((LORA_PAD, hgw), lambda b, g, t: (0, g)),
            pl.BlockSpec((GATE_LORA, hgw), lambda b, g, t: (0, g)),
            pl.BlockSpec((8, hgw), lambda b, g, t: (0, g)),
            pl.BlockSpec((hgw, hgw), lambda b, g, t: (0, 0)),
            pl.BlockSpec((tb, tb), lambda b, g, t: (0, 0)),
        ],
        out_specs=pl.BlockSpec((tb, hgw), lambda b, g, t: (row(b, g, t), g)),
        scratch_shapes=[
            pltpu.VMEM((8, hgw), F32), pltpu.VMEM((8, hgw), F32), pltpu.VMEM((8, hgw), F32),
            pltpu.VMEM((8, LORA_W), F32),
            pltpu.VMEM((hgw // LANE, LANE, LANE), F32),
        ],
        compiler_params=_cparams(("parallel", "parallel", "arbitrary")),
        name="rwkv",
    )(p, p, p, p, mu, mu, mu, mu, ww, wa, wg, par, e, ltri)


def _pool_kernel(u_ref, w_ref, s_ref, yb_ref, tail_ref):
    tb, pd = u_ref.shape
    gd = pd // len(POOL_WINDOWS)
    t = pl.program_id(1)

    @pl.when(t == 0)
    def _():
        tail_ref[...] = jnp.zeros_like(tail_ref)

    u = u_ref[...]
    ext = jnp.concatenate([tail_ref[...], u], axis=0)
    tail_ref[...] = u[tb - POOL_HALO:tb, :]
    pos = t * tb + lax.broadcasted_iota(jnp.int32, (tb, 1), 0) + 1
    for gi, win in enumerate(POOL_WINDOWS):
        ln = slice(gi * gd, (gi + 1) * gd)
        s = ext[:, ln]
        step = 1
        while step < win:
            s = s + pltpu.roll(s, step, axis=0)
            step *= 2
        cnt = jnp.minimum(pos, win).astype(F32)
        pooled = s[POOL_HALO:, :] / cnt - u[:, ln]
        mixed = _dot(pooled.astype(BF16), w_ref[gi])
        yb_ref[:, ln] = (mixed * s_ref[:, ln]).astype(BF16)


def _pool(p, w_pool, scale, batch, tp, pool_off):
    n = p.shape[0]
    ng, gd, _ = w_pool.shape
    pd = ng * gd
    assert ng == len(POOL_WINDOWS) and pool_off % pd == 0 and gd % LANE == 0
    assert all(w & (w - 1) == 0 and w <= POOL_HALO for w in POOL_WINDOWS)
    tb = _pick(tp, (384, 256, 128))
    ntb = tp // tb
    return pl.pallas_call(
        _pool_kernel,
        out_shape=jax.ShapeDtypeStruct((n, pd), BF16),
        grid=(batch, ntb),
        in_specs=[
            pl.BlockSpec((tb, pd), lambda b, t: (b * ntb + t, pool_off // pd)),
            pl.BlockSpec((ng, gd, gd), lambda b, t: (0, 0, 0)),
            pl.BlockSpec((1, pd), lambda b, t: (0, 0)),
        ],
        out_specs=pl.BlockSpec((tb, pd), lambda b, t: (b * ntb + t, 0)),
        scratch_shapes=[pltpu.VMEM((POOL_HALO, pd), F32)],
        compiler_params=_cparams(("parallel", "arbitrary")),
        name="pool",
    )(p, w_pool, scale)


def _merge_kernel(ya_ref, yb_ref, wa_ref, wb_ref, g0_ref, g1_ref, bg_ref, o_ref):
    g0 = jax.nn.sigmoid(g0_ref[...] + bg_ref[0:1, :])
    g1 = jax.nn.sigmoid(g1_ref[...] + bg_ref[1:2, :])
    o_ref[...] = (g0 * _dot(ya_ref[...], wa_ref[...]) + g1 * _dot(yb_ref[...], wb_ref[...])).astype(BF16)


def _merge(ya, yb, wa, wb, p, b_gate, gate_off):
    n, d = ya.shape
    pd = yb.shape[1]
    tm = _pick(n, (768, 512, 384, 256, 128))
    tn = _pick(d, (512, 256, 128))
    assert gate_off % tn == 0
    g0 = gate_off // tn
    g1 = (gate_off + d) // tn
    return pl.pallas_call(
        _merge_kernel,
        out_shape=jax.ShapeDtypeStruct((n, d), BF16),
        grid=(n // tm, d // tn),
        in_specs=[
            pl.BlockSpec((tm, d), lambda i, j: (i, 0)),
            pl.BlockSpec((tm, pd), lambda i, j: (i, 0)),
            pl.BlockSpec((d, tn), lambda i, j: (0, j)),
            pl.BlockSpec((pd, tn), lambda i, j: (0, j)),
            pl.BlockSpec((tm, tn), lambda i, j: (i, g0 + j)),
            pl.BlockSpec((tm, tn), lambda i, j: (i, g1 + j)),
            pl.BlockSpec((2, tn), lambda i, j: (0, j)),
        ],
        out_specs=pl.BlockSpec((tm, tn), lambda i, j: (i, j)),
        compiler_params=_cparams(("parallel", "arbitrary")),
        name="merge",
    )(ya, yb, wa, wb, p, p, b_gate)


def _outproj_kernel(alpha, m_ref, w_ref, h_ref, g_ref, b_ref, o_ref, ot_ref):
    y = alpha * h_ref[...] + _dot(m_ref[...], w_ref[...])
    h1 = _layer_norm(y, g_ref[...], b_ref[...])
    o_ref[...] = h1
    ot_ref[...] = h1.T.astype(BF16)


def _outproj_ln(merged, w_out, h0, g, b, alpha):
    n, d = merged.shape
    tm = _pick(n, (256, 128))
    return pl.pallas_call(
        functools.partial(_outproj_kernel, alpha),
        out_shape=(jax.ShapeDtypeStruct((n, d), F32), jax.ShapeDtypeStruct((d, n), BF16)),
        grid=(n // tm,),
        in_specs=[
            pl.BlockSpec((tm, d), lambda i: (i, 0)),
            pl.BlockSpec((d, d), lambda i: (0, 0)),
            pl.BlockSpec((tm, d), lambda i: (i, 0)),
            pl.BlockSpec((1, d), lambda i: (0, 0)),
            pl.BlockSpec((1, d), lambda i: (0, 0)),
        ],
        out_specs=(
            pl.BlockSpec((tm, d), lambda i: (i, 0)),
            pl.BlockSpec((d, tm), lambda i: (0, i)),
        ),
        compiler_params=_cparams(("parallel",)),
        name="outproj_ln",
    )(merged, w_out, h0, g, b)


def _top_k_ranks(s):
    key = lax.broadcasted_iota(jnp.int32, s.shape, 0).astype(F32)
    work = s
    rank = jnp.full(s.shape, NOT_TOP, F32)
    vals = []
    for r in range(PEER_TOPK):
        m = jnp.max(work, axis=0, keepdims=True)
        first = jnp.min(jnp.where(work == m, key, float(s.shape[0])), axis=0, keepdims=True)
        sel = key == first
        rank = jnp.where(sel, float(r + 1), rank)
        work = jnp.where(sel, -jnp.inf, work)
        vals.append(m)
    return rank, jnp.concatenate(vals, axis=0)


def _candidate_rows():
    pairs = [(0, j) for j in range(PEER_TOPK)]
    half = PEER_TOPK // 2
    for i in range(1, half):
        pairs += [(i, j) for j in range(half)]
    pairs += [(i, 0) for i in range(half, PEER_TOPK)]
    return pairs


def _route_kernel(h_ref, wq_ref, keys_ref, pos_ref, a_ref, c_ref, b_ref, r_ref, q_ref):
    tn = h_ref.shape[1]
    n_heads = a_ref.shape[0]
    half = PEER_TOPK // 2
    q_ref[...] = _dot(wq_ref[...], h_ref[...]).astype(BF16)
    pos = pos_ref[...]
    n_rows = pos.shape[0]
    slot = lax.broadcasted_iota(jnp.int32, (PEER_TOPK, tn), 0).astype(F32)

    def head(h, carry):
        base = pl.multiple_of(h * 2 * PEER_HALF, 2 * PEER_HALF)
        s1 = _dot(keys_ref[0], q_ref[pl.ds(base, PEER_HALF), :])
        s2 = _dot(keys_ref[1], q_ref[pl.ds(base + PEER_HALF, PEER_HALF), :])
        rank1, sv1 = _top_k_ranks(s1)
        rank2, sv2 = _top_k_ranks(s2)

        blocks = [sv1[0:1] + sv2]
        for i in range(1, half):
            blocks.append(sv1[i:i + 1] + sv2[0:half])
        blocks.append(sv1[half:PEER_TOPK] + sv2[0:1])
        cand = jnp.concatenate(blocks, axis=0)
        posb = jnp.broadcast_to(pos, (n_rows, tn))
        cnt = jnp.zeros((PEER_TOPK, tn), F32)
        z = jnp.ones((1, tn), F32)
        smax = None
        for it in range(PEER_TOPK):
            m = jnp.max(cand, axis=0, keepdims=True)
            first = jnp.min(jnp.where(cand == m, posb, float(PEER_TOPK * PEER_TOPK)), axis=0, keepdims=True)
            cand = jnp.where(posb == first, -jnp.inf, cand)
            cnt = cnt + jnp.where(slot == jnp.floor(first * (1.0 / PEER_TOPK)), 1.0, 0.0)
            if it == 0:
                smax = m
            else:
                z = z + jnp.exp(m - smax)

        ci = jnp.zeros(s1.shape, F32)
        for r in range(PEER_TOPK):
            ci = ci + jnp.where(rank1 == float(r + 1), cnt[r:r + 1], 0.0)
        a_ref[h] = jnp.where(rank1 <= PEER_TOPK, jnp.exp(s1 - sv1[0:1]), 0.0)
        c_ref[h] = ci
        b_ref[h] = jnp.where(rank2 <= PEER_TOPK, jnp.exp(s2 - sv2[0:1]) / z, 0.0).astype(BF16)
        r_ref[h] = rank2.astype(BF16)
        return carry

    lax.fori_loop(0, n_heads, head, 0)


def _peer_route(h1t, wq_t, keys):
    d, n = h1t.shape
    hq = wq_t.shape[0]
    n_heads = hq // (2 * PEER_HALF)
    tn = _pick(n, (256, 128))
    pairs = _candidate_rows()
    pos = jnp.asarray([[i * PEER_TOPK + j] for i, j in pairs], F32)
    shp = jax.ShapeDtypeStruct((n_heads, N_KEYS, n), F32)
    shp16 = jax.ShapeDtypeStruct((n_heads, N_KEYS, n), BF16)
    blk = pl.BlockSpec((n_heads, N_KEYS, tn), lambda i: (0, 0, i))
    return pl.pallas_call(
        _route_kernel,
        out_shape=(shp, shp, shp16, shp16),
        grid=(n // tn,),
        in_specs=[
            pl.BlockSpec((d, tn), lambda i: (0, i)),
            pl.BlockSpec((hq, d), lambda i: (0, 0)),
            pl.BlockSpec((2, N_KEYS, PEER_HALF), lambda i: (0, 0, 0)),
            pl.BlockSpec((len(pairs), 1), lambda i: (0, 0)),
        ],
        out_specs=(blk, blk, blk, blk),
        scratch_shapes=[pltpu.VMEM((hq, tn), BF16)],
        compiler_params=_cparams(("parallel",)),
        name="peer_route",
    )(h1t, wq_t, keys, pos)


def _gelu(x):
    return 0.5 * x * (1.0 + lax.erf(x * (1.0 / math.sqrt(2.0))))


def _dense_kernel(ht_ref, u_ref, v_ref, a_ref, c_ref, b_ref, r_ref, o_ref):
    k = pl.program_id(1)
    te, tn = u_ref.shape[0], ht_ref.shape[1]
    n_heads = a_ref.shape[0]
    sub = te // N_KEYS

    @pl.when(k == 0)
    def _():
        o_ref[...] = jnp.zeros_like(o_ref)

    sc = _dot(u_ref[...], ht_ref[...])
    zero = jnp.zeros((), BF16)
    acts = []
    for ii in range(sub):
        i = k * sub + ii
        w = None
        for h in range(n_heads):
            a_row = jnp.broadcast_to(a_ref[h, pl.ds(i, 1), :], (N_KEYS, tn)).astype(BF16)
            c_row = jnp.broadcast_to(c_ref[h, pl.ds(i, 1), :], (N_KEYS, tn)).astype(BF16)
            term = a_row * jnp.where(r_ref[h] <= c_row, b_ref[h], zero)
            w = term if w is None else w + term
        acts.append(_gelu(sc[ii * N_KEYS:(ii + 1) * N_KEYS]).astype(BF16) * w)
    act = jnp.concatenate(acts, axis=0)
    o_ref[...] += _dot_tn(act, v_ref[...])


def _peer_dense(h1t, u, v, a, c, b, r):
    d, n = h1t.shape
    ne = u.shape[0]
    n_heads = a.shape[0]
    tn = _pick(n, (768, 384, 256, 128))
    te = _pick(ne, (512, 256, 128))
    rblk = pl.BlockSpec((n_heads, N_KEYS, tn), lambda i, k: (0, 0, i))
    return pl.pallas_call(
        _dense_kernel,
        out_shape=jax.ShapeDtypeStruct((n, d), F32),
        grid=(n // tn, ne // te),
        in_specs=[
            pl.BlockSpec((d, tn), lambda i, k: (0, i)),
            pl.BlockSpec((te, d), lambda i, k: (k, 0)),
            pl.BlockSpec((te, d), lambda i, k: (k, 0)),
            rblk, rblk, rblk, rblk,
        ],
        out_specs=pl.BlockSpec((tn, d), lambda i, k: (i, 0)),
        compiler_params=_cparams(("parallel", "arbitrary")),
        name="peer_dense",
    )(h1t, u, v, a, c, b, r)


def _residual_ln_kernel(alpha, h_ref, y_ref, g_ref, b_ref, o_ref):
    o_ref[...] = _layer_norm(alpha * h_ref[...] + y_ref[...], g_ref[...], b_ref[...])


def _residual_ln(h, y, g, b, alpha):
    n, d = h.shape
    tm = _pick(n, (768, 512, 384, 256, 128))
    row = pl.BlockSpec((tm, d), lambda i: (i, 0))
    vec = pl.BlockSpec((1, d), lambda i: (0, 0))
    return pl.pallas_call(
        functools.partial(_residual_ln_kernel, alpha),
        out_shape=jax.ShapeDtypeStruct((n, d), F32),
        grid=(n // tm,),
        in_specs=[row, row, vec, vec],
        out_specs=row,
        compiler_params=_cparams(("parallel",)),
        name="residual_ln",
    )(h, y, g, b)


def _pad_rows(w, rows):
    return jnp.pad(w, ((0, rows - w.shape[0]), (0, 0)))


def kernel(x, meta_tokens, ln_in_g, ln_in_b, w_in, mu_shift, w0, w_w2, a0, w_a2, w_g2, k_k, k_a, r_k, gn_g, gn_b, w_pool, pool_scale, b_gate, w_branch_a, w_branch_b, w_out, ln1_g, ln1_b, w_q_peer, sub_keys, u_table, v_table, ln2_g, ln2_b):
    batch, seq, d = x.shape
    depth = w_in.shape[0]
    n_meta = meta_tokens.shape[0]
    t = n_meta + seq
    tp = -(-t // TIME_PAD) * TIME_PAD
    n = batch * tp
    pd = w_pool.shape[1] * w_pool.shape[2]
    assert d % LANE == 0 and w_w2.shape[1] == DECAY_LORA and w_a2.shape[1] == ICL_LORA
    assert w_g2.shape[1] == GATE_LORA and sub_keys.shape[2] == N_KEYS and sub_keys.shape[3] == PEER_HALF
    alpha = (2.0 * depth) ** 0.25

    meta = jnp.broadcast_to(meta_tokens[None].astype(x.dtype), (batch, n_meta, d))
    hcat = jnp.concatenate([meta, x, jnp.zeros((batch, tp - t, d), x.dtype)], axis=1).reshape(n, d)

    c_rkv = 3 * d
    c_wl = c_rkv
    c_al = c_wl + DECAY_LORA
    c_gl = c_al + ICL_LORA
    c_pool = c_gl + GATE_LORA
    c_gate = c_pool + pd
    pool_off = 3 * d
    gate_off = pool_off + pd
    lora_off = gate_off + 2 * d

    def relayout(w):
        z = jnp.zeros(w.shape[:-1] + (LORA_PAD - DECAY_LORA,), w.dtype)
        z2 = jnp.zeros(w.shape[:-1] + (LORA_PAD - ICL_LORA,), w.dtype)
        return jnp.concatenate([w[..., :c_rkv], w[..., c_pool:c_gate], w[..., c_gate:],
                                w[..., c_wl:c_al], z, w[..., c_al:c_gl], z2, w[..., c_gl:c_pool]], axis=-1)

    h = None
    for l in range(depth):
        w_in_p = relayout(w_in[l]).astype(BF16)
        mu_p = jnp.concatenate([mu_shift[l], jnp.zeros((w_in.shape[2] - mu_shift.shape[1],), F32)])
        mu_p = relayout(mu_p[None])
        if l == 0:
            h0, p = _ln_inproj(hcat, ln_in_g[None], ln_in_b[None], w_in_p)
        else:
            raise NotImplementedError("depth > 1")
        par = jnp.stack([w0[l], a0[l], k_k[l], k_a[l], r_k[l].reshape(-1), gn_g[l], gn_b[l],
                         jnp.zeros((d,), F32)])
        ya = _rwkv(p, mu_p, _pad_rows(w_w2[l], LORA_PAD).astype(BF16), _pad_rows(w_a2[l], LORA_PAD).astype(BF16),
                   w_g2[l].astype(BF16), par, batch, tp, d, lora_off)
        yb = _pool(p, w_pool[l].astype(BF16), pool_scale[l][None], batch, tp, pool_off)
        merged = _merge(ya, yb, w_branch_a[l].astype(BF16), w_branch_b[l].astype(BF16), p, b_gate[l], gate_off)
        h1, h1t = _outproj_ln(merged, w_out[l].astype(BF16), h0, ln1_g[l][None], ln1_b[l][None], alpha)
        ra, rc, rb, rr = _peer_route(h1t, w_q_peer[l].T.astype(BF16), sub_keys[l].astype(BF16))
        y = _peer_dense(h1t, u_table[l].astype(BF16), v_table[l].astype(BF16), ra, rc, rb, rr)
        h = _residual_ln(h1, y, ln2_g[l][None], ln2_b[l][None], alpha)
    return h.reshape(batch, tp, d)[:, n_meta:t]
```

```python
import functools
import math

import jax
import jax.numpy as jnp
from jax import lax
from jax.experimental import pallas as pl
from jax.experimental.pallas import tpu as pltpu

F32 = jnp.float32
BF16 = jnp.bfloat16

LANE = 128
N_META = 16
RWKV_HEAD = 64
CHUNK = 64
DECAY_LORA = 96
ICL_LORA = 96
GATE_LORA = 256
LORA_PAD = 128
LORA_W = 2 * LORA_PAD + GATE_LORA
DECAY_SCALE = math.exp(-0.5)
GN_EPS = 64e-5
LN_EPS = 1e-5
POOL_WINDOWS = (2, 4, 8, 16)
POOL_HALO = 16
PEER_TOPK = 16
N_KEYS = 128
PEER_HALF = 128
NOT_TOP = 99.0
DENSE_COLS = 256
TIME_PAD = 384
VMEM_LIMIT = 56 * 1024 * 1024


def _cparams(sem):
    return pltpu.CompilerParams(dimension_semantics=sem, vmem_limit_bytes=VMEM_LIMIT)


def _pick(n, cands):
    for c in cands:
        if n % c == 0:
            return c
    raise ValueError(f"no tile in {cands} divides {n}")


def _layer_norm(x, g, b):
    mu = jnp.mean(x, axis=-1, keepdims=True)
    xc = x - mu
    var = jnp.mean(xc * xc, axis=-1, keepdims=True)
    return xc * lax.rsqrt(var + LN_EPS) * g + b


def _dot(a, b):
    return jnp.dot(a, b, preferred_element_type=F32)


def _dot_nt(a, b):
    return lax.dot_general(a, b, (((1,), (1,)), ((), ())), preferred_element_type=F32)


def _dot_tn(a, b):
    return lax.dot_general(a, b, (((0,), (0,)), ((), ())), preferred_element_type=F32)


def _ln_inproj_kernel(starts, x_ref, g_ref, b_ref, *refs):
    n_w = len(starts) - 1
    w_refs, (h_ref, p_ref, hb_ref) = refs[:n_w], refs[n_w:]
    j = pl.program_id(1)

    @pl.when(j == 0)
    def _():
        h = _layer_norm(x_ref[...], g_ref[...], b_ref[...])
        h_ref[...] = h
        hb_ref[...] = h.astype(BF16)

    for s in range(n_w):
        @pl.when(jnp.logical_and(j >= starts[s], j < starts[s + 1]))
        def _(w_ref=w_refs[s]):
            p_ref[...] = _dot(hb_ref[...], w_ref[...]).astype(BF16)


def _ln_inproj(xcat, g, b, weights):
    n, d = xcat.shape
    widths = [w.shape[1] for w in weights]
    nc = sum(widths)
    tm = _pick(n, (768, 512, 384, 256, 128))
    tn = _pick(math.gcd(*widths), (512, 256, 128))
    starts = [0]
    for w in widths:
        starts.append(starts[-1] + w // tn)

    def w_spec(s):
        lo, hi = starts[s], starts[s + 1]
        return pl.BlockSpec((d, tn), lambda i, j: (0, jnp.clip(j - lo, 0, hi - lo - 1)))

    return pl.pallas_call(
        functools.partial(_ln_inproj_kernel, tuple(starts)),
        out_shape=(jax.ShapeDtypeStruct((n, d), F32), jax.ShapeDtypeStruct((n, nc), BF16)),
        grid=(n // tm, nc // tn),
        in_specs=[
            pl.BlockSpec((tm, d), lambda i, j: (i, 0)),
            pl.BlockSpec((1, d), lambda i, j: (0, 0)),
            pl.BlockSpec((1, d), lambda i, j: (0, 0)),
        ] + [w_spec(s) for s in range(len(weights))],
        out_specs=(
            pl.BlockSpec((tm, d), lambda i, j: (i, 0)),
            pl.BlockSpec((tm, tn), lambda i, j: (i, j)),
        ),
        scratch_shapes=[pltpu.VMEM((tm, d), BF16)],
        compiler_params=_cparams(("parallel", "arbitrary")),
        name="ln_inproj",
    )(xcat, g, b, *weights)


def _token_shift(x, prev_ref, mu):
    rows = x.shape[0]
    rolled = pltpu.roll(x, 1, axis=0)
    first = lax.broadcasted_iota(jnp.int32, x.shape, 0) == 0
    prev = jnp.where(first, prev_ref[0:1, :], rolled)
    prev_ref[0:1, :] = x[rows - 1:rows, :]
    return x + (prev - x) * mu


def _split3(x):
    h1 = x.astype(BF16)
    r1 = x - h1.astype(F32)
    h2 = r1.astype(BF16)
    h3 = (r1 - h2.astype(F32)).astype(BF16)
    return h1, h2, h3


def _seg_sum(x, e):
    return _dot(x.astype(BF16), e)


def _stack_heads(x, lane_lo):
    return jnp.concatenate([jnp.where(lane_lo, x, 0.0), jnp.where(lane_lo, 0.0, x)], axis=0)


def _pair_masks(c):
    lane = lax.broadcasted_iota(jnp.int32, (c, LANE), 1)
    row = lax.broadcasted_iota(jnp.int32, (c, LANE), 0)
    lane_lo = lane < RWKV_HEAD
    src = jnp.where(lane_lo, lane, lane - RWKV_HEAD)
    r2 = lax.broadcasted_iota(jnp.int32, (2 * c, LANE), 0)
    l2 = lax.broadcasted_iota(jnp.int32, (2 * c, LANE), 1)
    return lane_lo, src <= row, src < row, r2 == l2, (r2 < RWKV_HEAD) == (l2 < RWKV_HEAD)


def _chunk_prepare(rt, at, kt, bt, v):
    c = rt[0].shape[0]
    lane_lo, incl, strict, eye, _ = _pair_masks(c)
    stack = lambda x: _stack_heads(x, lane_lo)
    n = range(len(rt))

    lhs = [jnp.concatenate([rt[i], at[i]], axis=0).astype(BF16) for i in n]
    rhs = [jnp.concatenate([stack(kt[i]), stack(bt[i])], axis=0).astype(BF16) for i in n]
    p = [_dot_nt(lhs[i], rhs[i]) for i in n]
    arkb = [jnp.concatenate([jnp.where(incl, p[i][0:c, 0:LANE], 0.0),
                             jnp.where(incl, p[i][0:c, LANE:2 * LANE], 0.0)], axis=1).astype(BF16) for i in n]
    aak = [stack(jnp.where(strict, p[i][c:2 * c, 0:LANE], 0.0)).astype(BF16) for i in n]
    n_bd = [stack(jnp.where(strict, p[i][c:2 * c, LANE:2 * LANE], 0.0)) for i in n]

    x = [jnp.where(eye, 1.0, 0.0) + n_bd[i] for i in n]
    nb = [n_bd[i].astype(BF16) for i in n]
    pw = [_dot(nb[i], nb[i]) for i in n]
    for _ in range(int(math.log2(c)) - 2):
        res = [_dot(pw[i].astype(BF16), jnp.concatenate([pw[i], x[i]], axis=1).astype(BF16)) for i in n]
        pw = [res[i][:, 0:LANE] for i in n]
        x = [x[i] + res[i][:, LANE:2 * LANE] for i in n]
    x = [x[i] + _dot(pw[i].astype(BF16), x[i].astype(BF16)) for i in n]

    v2 = [stack(v[i]).astype(BF16) for i in n]
    av = [_dot(aak[i], v2[i]) for i in n]
    res = [_dot(x[i].astype(BF16), jnp.concatenate([av[i], stack(at[i])], axis=1).astype(BF16)) for i in n]
    u0 = [res[i][:, 0:LANE] for i in n]
    ahat = [res[i][:, LANE:2 * LANE] for i in n]
    return arkb, v2, u0, ahat


def _chunk_advance(st, rt, v, khat, bhat, wtot, arkb, v2, u0, ahat):
    c = rt[0].shape[0]
    same_head = _pair_masks(c)[4]
    n = range(len(rt))
    x1 = [_dot_nt(jnp.concatenate([ahat[i], rt[i]], axis=0).astype(BF16), st[i].astype(BF16)) for i in n]
    us = [x1[i][0:2 * c] + u0[i] for i in n]
    y = [x1[i][2 * c:3 * c] + _dot(arkb[i], jnp.concatenate([v2[i], us[i].astype(BF16)], axis=0)) for i in n]
    u = [us[i][0:c] + us[i][c:2 * c] for i in n]
    upd = [_dot_tn(jnp.concatenate([v[i], u[i]], axis=0).astype(BF16),
                   jnp.concatenate([khat[i], bhat[i]], axis=0).astype(BF16)) for i in n]
    st_new = [st[i] * wtot[i] + jnp.where(same_head, upd[i], 0.0) for i in n]
    return y, st_new


def _rwkv_kernel(pr_ref, pk_ref, pv_ref, pl_ref, mur_ref, muk_ref, muv_ref, mul_ref,
                 ww_ref, wa_ref, wg_ref, par_ref, e_ref, ltri_ref,
                 ya_ref,
                 prev_r, prev_k, prev_v, prev_l, st_ref):
    tb, hgw = pr_ref.shape
    n_pairs = hgw // LANE
    n_chunks = tb // CHUNK

    @pl.when(pl.program_id(2) == 0)
    def _():
        prev_r[...] = jnp.zeros_like(prev_r)
        prev_k[...] = jnp.zeros_like(prev_k)
        prev_v[...] = jnp.zeros_like(prev_v)
        prev_l[...] = jnp.zeros_like(prev_l)
        st_ref[...] = jnp.zeros_like(st_ref)

    r = _token_shift(pr_ref[...].astype(F32), prev_r, mur_ref[...])
    k = _token_shift(pk_ref[...].astype(F32), prev_k, muk_ref[...])
    v = _token_shift(pv_ref[...].astype(F32), prev_v, muv_ref[...])
    lo = _token_shift(pl_ref[...].astype(F32), prev_l, mul_ref[...])

    w0 = par_ref[0:1, :]
    a0 = par_ref[1:2, :]
    k_k = par_ref[2:3, :]
    k_a = par_ref[3:4, :]
    r_k = par_ref[4:5, :]
    gn_g = par_ref[5:6, :]
    gn_b = par_ref[6:7, :]
    e = e_ref[...]

    wl = jnp.tanh(lo[:, 0:LORA_PAD]).astype(BF16)
    al = lo[:, LORA_PAD:2 * LORA_PAD].astype(BF16)
    gl = jax.nn.sigmoid(lo[:, 2 * LORA_PAD:LORA_W]).astype(BF16)
    logw = -DECAY_SCALE * jax.nn.sigmoid(w0 + _dot(wl, ww_ref[...]))
    a = jax.nn.sigmoid(a0 + _dot(al, wa_ref[...]))
    g = _dot(gl, wg_ref[...])

    kkr = k * k_k
    kk = kkr / jnp.maximum(jnp.sqrt(_seg_sum(kkr * kkr, e)), 1e-12)
    k2 = k * (1.0 + (a - 1.0) * k_a)
    bonus = _seg_sum(r * k2 * r_k, e) * v
    beta = kk * a

    ltri = ltri_ref[...]
    h1, h2, h3 = _split3(logw)
    cum = _dot(ltri, h1) + _dot(ltri, h2) + _dot(ltri, h3)

    rt_all = r * jnp.exp(cum)
    inv = jnp.exp(-cum)
    kt_all = k2 * inv
    bt_all = beta * inv
    at_all = -kk * jnp.exp(cum - logw)

    tiles = lambda z: [z[ci * CHUNK:(ci + 1) * CHUNK, q * LANE:(q + 1) * LANE]
                       for ci in range(n_chunks) for q in range(n_pairs)]
    rt_t, v_t = tiles(rt_all), tiles(v)
    arkb, v2, u0, ahat = _chunk_prepare(rt_t, tiles(at_all), tiles(kt_all), tiles(bt_all), v_t)

    st = [st_ref[q] for q in range(n_pairs)]
    y_rows = []
    for ci in range(n_chunks):
        rows = slice(ci * CHUNK, (ci + 1) * CHUNK)
        tot = cum[(ci + 1) * CHUNK - 1:(ci + 1) * CHUNK, :]
        tail = jnp.exp(tot - cum[rows])
        khat_c = k2[rows] * tail
        bhat_c = beta[rows] * tail
        wtot_c = jnp.exp(tot)
        pair = lambda z: [z[:, q * LANE:(q + 1) * LANE] for q in range(n_pairs)]
        sel = slice(ci * n_pairs, (ci + 1) * n_pairs)
        y_c, st = _chunk_advance(st, rt_t[sel], v_t[sel], pair(khat_c), pair(bhat_c), pair(wtot_c),
                                 arkb[sel], v2[sel], u0[sel], ahat[sel])
        y_rows.append(jnp.concatenate(y_c, axis=1))
    for q in range(n_pairs):
        st_ref[q] = st[q]

    y = jnp.concatenate(y_rows, axis=0)
    inv_n = 1.0 / RWKV_HEAD
    ym = _seg_sum(y, e) * inv_n
    yc = y - ym
    yv = _seg_sum(yc * yc, e) * inv_n
    yn = yc * lax.rsqrt(yv + GN_EPS) * gn_g + gn_b
    ya_ref[...] = ((yn + bonus) * g).astype(BF16)


def _rwkv(p, mu, ww, wa, wg, par, batch, tp, d, lora_off):
    n = p.shape[0]
    hgw = 512 if d % 512 == 0 else d
    tb = _pick(tp, (384, 128, 64))
    ntb = tp // tb
    groups = d // hgw
    koff, voff = d // hgw, 2 * d // hgw
    loff = lora_off // LORA_W
    assert lora_off % LORA_W == 0

    head_id = jnp.arange(hgw) // RWKV_HEAD
    e = (head_id[:, None] == head_id[None, :]).astype(BF16)
    t_id = jnp.arange(tb)
    ltri = ((t_id[:, None] >= t_id[None, :]) &
            (t_id[:, None] // CHUNK == t_id[None, :] // CHUNK)).astype(BF16)

    row = lambda b, g, t: b * ntb + t
    return pl.pallas_call(
        _rwkv_kernel,
        out_shape=jax.ShapeDtypeStruct((n, d), BF16),
        grid=(batch, groups, ntb),
        in_specs=[
            pl.BlockSpec((tb, hgw), lambda b, g, t: (row(b, g, t), g)),
            pl.BlockSpec((tb, hgw), lambda b, g, t: (row(b, g, t), koff + g)),
            pl.BlockSpec((tb, hgw), lambda b, g, t: (row(b, g, t), voff + g)),
            pl.BlockSpec((tb, LORA_W), lambda b, g, t: (row(b, g, t), loff)),
            pl.BlockSpec((1, hgw), lambda b, g, t: (0, g)),
            pl.BlockSpec((1, hgw), lambda b, g, t: (0, koff + g)),
            pl.BlockSpec((1, hgw), lambda b, g, t: (0, voff + g)),
            pl.BlockSpec((1, LORA_W), lambda b, g, t: (0, loff)),
            pl.BlockSpec((LORA_PAD, hgw), lambda b, g, t: (0, g)),
            pl.BlockSpec((LORA_PAD, hgw), lambda b, g, t: (0, g)),
            pl.BlockSpec((GATE_LORA, hgw), lambda b, g, t: (0, g)),
            pl.BlockSpec((8, hgw), lambda b, g, t: (0, g)),
            pl.BlockSpec((hgw, hgw), lambda b, g, t: (0, 0)),
            pl.BlockSpec((tb, tb), lambda b, g, t: (0, 0)),
        ],
        out_specs=pl.BlockSpec((tb, hgw), lambda b, g, t: (row(b, g, t), g)),
        scratch_shapes=[
            pltpu.VMEM((8, hgw), F32), pltpu.VMEM((8, hgw), F32), pltpu.VMEM((8, hgw), F32),
            pltpu.VMEM((8, LORA_W), F32),
            pltpu.VMEM((hgw // LANE, LANE, LANE), F32),
        ],
        compiler_params=_cparams(("parallel", "parallel", "arbitrary")),
        name="rwkv",
    )(p, p, p, p, mu, mu, mu, mu, ww, wa, wg, par, e, ltri)


def _pool_kernel(u_ref, w_ref, s_ref, yb_ref, tail_ref):
    tb, pd = u_ref.shape
    gd = pd // len(POOL_WINDOWS)
    t = pl.program_id(1)

    @pl.when(t == 0)
    def _():
        tail_ref[...] = jnp.zeros_like(tail_ref)

    u = u_ref[...].astype(F32)
    ext = jnp.concatenate([tail_ref[...], u], axis=0)
    tail_ref[...] = u[tb - POOL_HALO:tb, :]
    pos = t * tb + lax.broadcasted_iota(jnp.int32, (tb, 1), 0) + 1
    for gi, win in enumerate(POOL_WINDOWS):
        ln = slice(gi * gd, (gi + 1) * gd)
        s = ext[:, ln]
        step = 1
        while step < win:
            s = s + pltpu.roll(s, step, axis=0)
            step *= 2
        cnt = jnp.minimum(pos, win).astype(F32)
        pooled = s[POOL_HALO:, :] / cnt - u[:, ln]
        mixed = _dot(pooled.astype(BF16), w_ref[gi])
        yb_ref[:, ln] = (mixed * s_ref[:, ln]).astype(BF16)


def _pool(p, w_pool, scale, batch, tp, pool_off):
    n = p.shape[0]
    ng, gd, _ = w_pool.shape
    pd = ng * gd
    assert ng == len(POOL_WINDOWS) and pool_off % pd == 0 and gd % LANE == 0
    assert all(w & (w - 1) == 0 and w <= POOL_HALO for w in POOL_WINDOWS)
    tb = _pick(tp, (384, 256, 128))
    ntb = tp // tb
    return pl.pallas_call(
        _pool_kernel,
        out_shape=jax.ShapeDtypeStruct((n, pd), BF16),
        grid=(batch, ntb),
        in_specs=[
            pl.BlockSpec((tb, pd), lambda b, t: (b * ntb + t, pool_off // pd)),
            pl.BlockSpec((ng, gd, gd), lambda b, t: (0, 0, 0)),
            pl.BlockSpec((1, pd), lambda b, t: (0, 0)),
        ],
        out_specs=pl.BlockSpec((tb, pd), lambda b, t: (b * ntb + t, 0)),
        scratch_shapes=[pltpu.VMEM((POOL_HALO, pd), F32)],
        compiler_params=_cparams(("parallel", "arbitrary")),
        name="pool",
    )(p, w_pool, scale)


def _merge_kernel(ya_ref, yb_ref, wa_ref, wb_ref, g0_ref, g1_ref, bg_ref, o_ref):
    g0 = jax.nn.sigmoid(g0_ref[...].astype(F32) + bg_ref[0:1, :])
    g1 = jax.nn.sigmoid(g1_ref[...].astype(F32) + bg_ref[1:2, :])
    o_ref[...] = (g0 * _dot(ya_ref[...], wa_ref[...]) + g1 * _dot(yb_ref[...], wb_ref[...])).astype(BF16)


def _merge(ya, yb, wa, wb, p, b_gate, gate_off):
    n, d = ya.shape
    pd = yb.shape[1]
    tm = _pick(n, (768, 512, 384, 256, 128))
    tn = _pick(d, (512, 256, 128))
    assert gate_off % tn == 0
    g0 = gate_off // tn
    g1 = (gate_off + d) // tn
    return pl.pallas_call(
        _merge_kernel,
        out_shape=jax.ShapeDtypeStruct((n, d), BF16),
        grid=(n // tm, d // tn),
        in_specs=[
            pl.BlockSpec((tm, d), lambda i, j: (i, 0)),
            pl.BlockSpec((tm, pd), lambda i, j: (i, 0)),
            pl.BlockSpec((d, tn), lambda i, j: (0, j)),
            pl.BlockSpec((pd, tn), lambda i, j: (0, j)),
            pl.BlockSpec((tm, tn), lambda i, j: (i, g0 + j)),
            pl.BlockSpec((tm, tn), lambda i, j: (i, g1 + j)),
            pl.BlockSpec((2, tn), lambda i, j: (0, j)),
        ],
        out_specs=pl.BlockSpec((tm, tn), lambda i, j: (i, j)),
        compiler_params=_cparams(("parallel", "arbitrary")),
        name="merge",
    )(ya, yb, wa, wb, p, p, b_gate)


def _outproj_kernel(alpha, m_ref, w_ref, h_ref, g_ref, b_ref, o_ref, ot_ref):
    y = alpha * h_ref[...] + _dot(m_ref[...], w_ref[...])
    h1 = _layer_norm(y, g_ref[...], b_ref[...])
    o_ref[...] = h1
    ot_ref[...] = h1.T.astype(BF16)


def _outproj_ln(merged, w_out, h0, g, b, alpha):
    n, d = merged.shape
    tm = _pick(n, (384, 256, 128))
    return pl.pallas_call(
        functools.partial(_outproj_kernel, alpha),
        out_shape=(jax.ShapeDtypeStruct((n, d), F32), jax.ShapeDtypeStruct((d, n), BF16)),
        grid=(n // tm,),
        in_specs=[
            pl.BlockSpec((tm, d), lambda i: (i, 0)),
            pl.BlockSpec((d, d), lambda i: (0, 0)),
            pl.BlockSpec((tm, d), lambda i: (i, 0)),
            pl.BlockSpec((1, d), lambda i: (0, 0)),
            pl.BlockSpec((1, d), lambda i: (0, 0)),
        ],
        out_specs=(
            pl.BlockSpec((tm, d), lambda i: (i, 0)),
            pl.BlockSpec((d, tm), lambda i: (0, i)),
        ),
        compiler_params=_cparams(("parallel",)),
        name="outproj_ln",
    )(merged, w_out, h0, g, b)


def _extract_top_k(scores, break_ties):
    shape = scores[0].shape
    key = lax.broadcasted_iota(jnp.int32, shape, 0).astype(F32)
    n = range(len(scores))
    work = list(scores)
    rank = [jnp.full(shape, NOT_TOP, F32) for _ in n]
    vals = [[] for _ in n]
    for r in range(PEER_TOPK):
        m = [jnp.max(work[i], axis=0, keepdims=True) for i in n]
        if break_ties:
            first = [jnp.min(jnp.where(work[i] == m[i], key, float(shape[0])), axis=0, keepdims=True) for i in n]
            sel = [key == first[i] for i in n]
        else:
            sel = [work[i] == m[i] for i in n]
        rank = [jnp.where(sel[i], float(r + 1), rank[i]) for i in n]
        work = [jnp.where(sel[i], -jnp.inf, work[i]) for i in n]
        for i in n:
            vals[i].append(m[i])
    out = []
    for i in n:
        out += [rank[i], jnp.concatenate(vals[i], axis=0)]
    return out


def _top_k_ranks(scores):
    fast = _extract_top_k(scores, False)
    ok = None
    for rank in fast[0::2]:
        taken = jnp.sum(jnp.where(rank <= PEER_TOPK, 1.0, 0.0), axis=0, keepdims=True)
        good = jnp.min(jnp.where(taken == float(PEER_TOPK), 1.0, 0.0)) > 0.5
        ok = good if ok is None else jnp.logical_and(ok, good)
    return lax.cond(ok, lambda: fast, lambda: _extract_top_k(scores, True))


def _candidate_rows():
    pairs = [(0, j) for j in range(PEER_TOPK)]
    half = PEER_TOPK // 2
    for i in range(1, half):
        pairs += [(i, j) for j in range(half)]
    pairs += [(i, 0) for i in range(half, PEER_TOPK)]
    return pairs


def _route_kernel(h_ref, wq_ref, keys_ref, pos_ref, a_ref, c_ref, b_ref, r_ref, q_ref):
    tn = h_ref.shape[1]
    n_heads = a_ref.shape[0]
    half = PEER_TOPK // 2
    q_ref[...] = _dot(wq_ref[...], h_ref[...]).astype(BF16)
    pos = pos_ref[...]
    n_rows = pos.shape[0]
    slot = lax.broadcasted_iota(jnp.int32, (PEER_TOPK, tn), 0).astype(F32)
    posb = jnp.broadcast_to(pos, (n_rows, tn))

    def route(s1, s2):
        rank1, sv1, rank2, sv2 = _top_k_ranks([s1, s2])

        blocks = [sv1[0:1] + sv2]
        for i in range(1, half):
            blocks.append(sv1[i:i + 1] + sv2[0:half])
        blocks.append(sv1[half:PEER_TOPK] + sv2[0:1])
        cand = jnp.concatenate(blocks, axis=0)
        cnt = jnp.zeros((PEER_TOPK, tn), F32)
        z = jnp.ones((1, tn), F32)
        smax = None
        for it in range(PEER_TOPK):
            m = jnp.max(cand, axis=0, keepdims=True)
            first = jnp.min(jnp.where(cand == m, posb, float(PEER_TOPK * PEER_TOPK)), axis=0, keepdims=True)
            cand = jnp.where(posb == first, -jnp.inf, cand)
            cnt = cnt + jnp.where(slot == jnp.floor(first * (1.0 / PEER_TOPK)), 1.0, 0.0)
            if it == 0:
                smax = m
            else:
                z = z + jnp.exp(m - smax)

        rank1_b = rank1.astype(BF16)
        cnt_b = cnt.astype(BF16)
        ci = jnp.zeros(s1.shape, BF16)
        for r in range(PEER_TOPK):
            ci = ci + jnp.where(rank1_b == float(r + 1), jnp.broadcast_to(cnt_b[r:r + 1], s1.shape),
                                jnp.zeros((), BF16))
        a = jnp.where(rank1 <= PEER_TOPK, jnp.exp(s1 - sv1[0:1]), 0.0)
        bz = jnp.where(rank2 <= PEER_TOPK, jnp.exp(s2 - sv2[0:1]) / z, 0.0)
        return a, ci.astype(F32), bz.astype(BF16), rank2.astype(BF16)

    def head(h, carry):
        base = pl.multiple_of(h * 2 * PEER_HALF, 2 * PEER_HALF)
        s1 = _dot(keys_ref[0], q_ref[pl.ds(base, PEER_HALF), :])
        s2 = _dot(keys_ref[1], q_ref[pl.ds(base + PEER_HALF, PEER_HALF), :])
        a_ref[h], c_ref[h], b_ref[h], r_ref[h] = route(s1, s2)
        return carry

    lax.fori_loop(0, n_heads, head, 0)


def _peer_route(h1t, wq_t, keys):
    d, n = h1t.shape
    hq = wq_t.shape[0]
    n_heads = hq // (2 * PEER_HALF)
    tn = _pick(n, (256, 128))
    pairs = _candidate_rows()
    pos = jnp.asarray([[i * PEER_TOPK + j] for i, j in pairs], F32)
    shp = jax.ShapeDtypeStruct((n_heads, N_KEYS, n), F32)
    shp16 = jax.ShapeDtypeStruct((n_heads, N_KEYS, n), BF16)
    blk = pl.BlockSpec((n_heads, N_KEYS, tn), lambda i: (0, 0, i))
    return pl.pallas_call(
        _route_kernel,
        out_shape=(shp, shp, shp16, shp16),
        grid=(n // tn,),
        in_specs=[
            pl.BlockSpec((d, tn), lambda i: (0, i)),
            pl.BlockSpec((hq, d), lambda i: (0, 0)),
            pl.BlockSpec((2, N_KEYS, PEER_HALF), lambda i: (0, 0, 0)),
            pl.BlockSpec((len(pairs), 1), lambda i: (0, 0)),
        ],
        out_specs=(blk, blk, blk, blk),
        scratch_shapes=[pltpu.VMEM((hq, tn), BF16)],
        compiler_params=_cparams(("parallel",)),
        name="peer_route",
    )(h1t, wq_t, keys, pos)


def _gelu(x):
    return 0.5 * x * (1.0 + lax.erf(x * (1.0 / math.sqrt(2.0))))


def _dense_kernel(ht_ref, u_ref, vt_ref, a_ref, c_ref, b_ref, r_ref, o_ref):
    k = pl.program_id(1)
    te, tn = u_ref.shape[0], ht_ref.shape[1]
    n_heads = a_ref.shape[0]
    sub = te // N_KEYS
    cb = DENSE_COLS if tn % DENSE_COLS == 0 else tn

    @pl.when(k == 0)
    def _():
        o_ref[...] = jnp.zeros_like(o_ref)

    zero = jnp.zeros((), BF16)
    sc_all = _dot(u_ref[...], ht_ref[...])
    for c in range(tn // cb):
        cols = slice(c * cb, (c + 1) * cb)
        sc = sc_all[:, cols]
        acts = []
        for ii in range(sub):
            w = None
            for h in range(n_heads):
                a_row = jnp.broadcast_to(a_ref[h, ii:ii + 1, cols], (N_KEYS, cb)).astype(BF16)
                c_row = jnp.broadcast_to(c_ref[h, ii:ii + 1, cols], (N_KEYS, cb)).astype(BF16)
                term = a_row * jnp.where(r_ref[h, :, cols] <= c_row, b_ref[h, :, cols], zero)
                w = term if w is None else w + term
            acts.append(_gelu(sc[ii * N_KEYS:(ii + 1) * N_KEYS]).astype(BF16) * w)
        act = jnp.concatenate(acts, axis=0)
        o_ref[:, cols] += _dot(vt_ref[...], act)


def _peer_dense(h1t, u, vt, a, c, b, r):
    d, n = h1t.shape
    ne = u.shape[0]
    n_heads = a.shape[0]
    tn = _pick(n, (768, 512, 256, 128))
    te = 8 * N_KEYS
    assert ne % te == 0
    fblk = pl.BlockSpec((n_heads, te // N_KEYS, tn), lambda i, k: (0, k, i))
    rblk = pl.BlockSpec((n_heads, N_KEYS, tn), lambda i, k: (0, 0, i))
    return pl.pallas_call(
        _dense_kernel,
        out_shape=jax.ShapeDtypeStruct((d, n), F32),
        grid=(n // tn, ne // te),
        in_specs=[
            pl.BlockSpec((d, tn), lambda i, k: (0, i)),
            pl.BlockSpec((te, d), lambda i, k: (k, 0)),
            pl.BlockSpec((d, te), lambda i, k: (0, k)),
            fblk, fblk, rblk, rblk,
        ],
        out_specs=pl.BlockSpec((d, tn), lambda i, k: (0, i)),
        compiler_params=_cparams(("parallel", "arbitrary")),
        name="peer_dense",
    )(h1t, u, vt, a, c, b, r)


def _residual_ln_kernel(alpha, h_ref, yt_ref, g_ref, b_ref, o_ref):
    o_ref[...] = _layer_norm(alpha * h_ref[...] + yt_ref[...].T, g_ref[...], b_ref[...])


def _residual_ln(h, yt, g, b, alpha):
    n, d = h.shape
    tm = _pick(n, (256, 128))
    row = pl.BlockSpec((tm, d), lambda i: (i, 0))
    vec = pl.BlockSpec((1, d), lambda i: (0, 0))
    return pl.pallas_call(
        functools.partial(_residual_ln_kernel, alpha),
        out_shape=jax.ShapeDtypeStruct((n, d), F32),
        grid=(n // tm,),
        in_specs=[row, pl.BlockSpec((d, tm), lambda i: (0, i)), vec, vec],
        out_specs=row,
        compiler_params=_cparams(("parallel",)),
        name="residual_ln",
    )(h, yt, g, b)


def _pad_rows(w, rows):
    return jnp.pad(w, ((0, rows - w.shape[0]), (0, 0)))


def kernel(x, meta_tokens, ln_in_g, ln_in_b, w_in, mu_shift, w0, w_w2, a0, w_a2, w_g2, k_k, k_a, r_k, gn_g, gn_b, w_pool, pool_scale, b_gate, w_branch_a, w_branch_b, w_out, ln1_g, ln1_b, w_q_peer, sub_keys, u_table, v_table, ln2_g, ln2_b):
    batch, seq, d = x.shape
    depth = w_in.shape[0]
    n_meta = meta_tokens.shape[0]
    t = n_meta + seq
    tp = -(-t // TIME_PAD) * TIME_PAD
    n = batch * tp
    pd = w_pool.shape[1] * w_pool.shape[2]
    assert d % LANE == 0 and w_w2.shape[1] == DECAY_LORA and w_a2.shape[1] == ICL_LORA
    assert w_g2.shape[1] == GATE_LORA and sub_keys.shape[2] == N_KEYS and sub_keys.shape[3] == PEER_HALF
    alpha = (2.0 * depth) ** 0.25

    meta = jnp.broadcast_to(meta_tokens[None].astype(x.dtype), (batch, n_meta, d))
    hcat = jnp.concatenate([meta, x, jnp.zeros((batch, tp - t, d), x.dtype)], axis=1).reshape(n, d)

    c_rkv = 3 * d
    c_wl = c_rkv
    c_al = c_wl + DECAY_LORA
    c_gl = c_al + ICL_LORA
    c_pool = c_gl + GATE_LORA
    c_gate = c_pool + pd
    gate_off = 3 * d
    pool_off = gate_off + 2 * d
    lora_off = pool_off + pd

    def small_slabs(w):
        z = jnp.zeros(w.shape[:-1] + (LORA_PAD - DECAY_LORA,), w.dtype)
        z2 = jnp.zeros(w.shape[:-1] + (LORA_PAD - ICL_LORA,), w.dtype)
        return jnp.concatenate([w[..., c_pool:c_gate], w[..., c_wl:c_al], z, w[..., c_al:c_gl], z2,
                                w[..., c_gl:c_pool]], axis=-1)

    h = None
    for l in range(depth):
        mu_p = jnp.concatenate([mu_shift[l], jnp.zeros((w_in.shape[2] - mu_shift.shape[1],), F32)])[None]
        mu_p = jnp.concatenate([mu_p[:, :c_rkv], mu_p[:, c_gate:], small_slabs(mu_p)], axis=-1)
        if l == 0:
            h0, p = _ln_inproj(hcat, ln_in_g[None], ln_in_b[None],
                               [w_in[l][:, :c_rkv].astype(BF16), w_in[l][:, c_gate:].astype(BF16),
                                small_slabs(w_in[l]).astype(BF16)])
        else:
            raise NotImplementedError("depth > 1")
        par = jnp.stack([w0[l], a0[l], k_k[l], k_a[l], r_k[l].reshape(-1), gn_g[l], gn_b[l],
                         jnp.zeros((d,), F32)])
        ya = _rwkv(p, mu_p, _pad_rows(w_w2[l], LORA_PAD).astype(BF16), _pad_rows(w_a2[l], LORA_PAD).astype(BF16),
                   w_g2[l].astype(BF16), par, batch, tp, d, lora_off)
        yb = _pool(p, w_pool[l].astype(BF16), pool_scale[l][None], batch, tp, pool_off)
        merged = _merge(ya, yb, w_branch_a[l].astype(BF16), w_branch_b[l].astype(BF16), p, b_gate[l], gate_off)
        h1, h1t = _outproj_ln(merged, w_out[l].astype(BF16), h0, ln1_g[l][None], ln1_b[l][None], alpha)
        ra, rc, rb, rr = _peer_route(h1t, w_q_peer[l].T.astype(BF16), sub_keys[l].astype(BF16))
        yt = _peer_dense(h1t, u_table[l].astype(BF16), v_table[l].T.astype(BF16), ra, rc, rb, rr)
        h = _residual_ln(h1, yt, ln2_g[l][None], ln2_b[l][None], alpha)
    return h.reshape(batch, tp, d)[:, n_meta:t]
```
